```python
import math
import jax, jax.numpy as jnp
from jax import lax
import numpy as np

D_MODEL = 1024
BATCH = 2
SEQ = 8192
DEPTH = 1

D_MIX = D_MODEL
A_HEADS = 4
A_HEAD_DIM = 128
A_WIDTH = A_HEADS * A_HEAD_DIM
SGU_CHUNK = 128
B_HEADS = 4
B_HEAD_V = 128
B_HEAD_K = 64
B_WIDTH_V = B_HEADS * B_HEAD_V
B_WIDTH_K = B_HEADS * B_HEAD_K
GATE_RANK = 16
GATE_TAU = 16.0
GLA_CHUNK = 64
OFF_U = 0
OFF_V = OFF_U + A_WIDTH
OFF_Q = OFF_V + A_WIDTH
OFF_K = OFF_Q + B_WIDTH_K
OFF_BV = OFF_K + B_WIDTH_K
OFF_G = OFF_BV + B_WIDTH_V
OFF_LR = OFF_G + B_WIDTH_V
D_IN = OFF_LR + GATE_RANK
D_FF = 2816
CONV_K = 3
EPS = 1e-6

kernel_name = "hybrid_sgu_gla_convffn_block"


def rms_norm(x, g):
    xf = x.astype(jnp.float32)
    y = xf * lax.rsqrt(jnp.mean(xf * xf, axis=-1, keepdims=True) + EPS)
    return (y * g.astype(jnp.float32)).astype(x.dtype)


def layer_norm(x, g, b):
    xf = x.astype(jnp.float32)
    mu = jnp.mean(xf, axis=-1, keepdims=True)
    var = jnp.mean(jnp.square(xf - mu), axis=-1, keepdims=True)
    y = (xf - mu) * lax.rsqrt(var + EPS)
    return (y * g.astype(jnp.float32) + b.astype(jnp.float32)).astype(x.dtype)


def sgu_mixer(u, v, ln_g, ln_b, w_s, b_s):
    bsz, s, _ = v.shape
    v = layer_norm(v, ln_g, ln_b)
    vc = v.reshape(bsz, s // SGU_CHUNK, SGU_CHUNK, A_HEADS, A_HEAD_DIM)
    tril = jnp.tril(jnp.ones((SGU_CHUNK, SGU_CHUNK), dtype=w_s.dtype))
    w_m = w_s * tril[None]
    z = jnp.einsum('hts,bcshd->bcthd', w_m, vc)
    z = z + jnp.transpose(b_s)[None, None, :, :, None]
    return u * z.reshape(bsz, s, A_WIDTH)


def gla_mixer(q, k, v, gk):
    bsz, s, h, dk = q.shape
    dv = v.shape[-1]
    n = s // GLA_CHUNK
    q = q * (dk ** -0.5)

    def to_chunks(t):
        t = t.reshape(bsz, n, GLA_CHUNK, h, t.shape[-1])
        return jnp.moveaxis(t, 1, 0)

    qs, ks, vs, gs = to_chunks(q), to_chunks(k), to_chunks(v), to_chunks(gk.astype(jnp.float32))
    mask = jnp.tril(jnp.ones((GLA_CHUNK, GLA_CHUNK), dtype=bool))[None, :, :, None, None]

    def step(state, inp):
        qc, kc, vc, gc = inp
        bcum = jnp.cumsum(gc, axis=1)
        b_last = bcum[:, -1]
        o_inter = jnp.einsum('bihk,bhkv->bihv', qc * jnp.exp(bcum), state)
        diff = bcum[:, :, None] - bcum[:, None, :]
        decay = jnp.exp(jnp.where(mask, diff, -jnp.inf))
        attn = jnp.einsum('bihk,bjhk,bijhk->bhij', qc, kc, decay)
        o_intra = jnp.einsum('bhij,bjhv->bihv', attn, vc)
        k_dec = kc * jnp.exp(b_last[:, None] - bcum)
        state = jnp.exp(b_last)[..., None] * state + jnp.einsum('bjhk,bjhv->bhkv', k_dec, vc)
        return state, o_inter + o_intra

    s0 = jnp.zeros((bsz, h, dk, dv), dtype=jnp.float32)
    _, o = lax.scan(step, s0, (qs, ks, vs, gs))
    return jnp.moveaxis(o, 0, 1).reshape(bsz, s, h, dv)


def causal_dwconv(h, w, b):
    s = h.shape[1]
    hp = jnp.pad(h, ((0, 0), (CONV_K - 1, 0), (0, 0)))
    y = b
    for i in range(CONV_K):
        y = y + hp[:, i:i + s] * w[i]
    return y


def setup_inputs(seed: int = 0) -> dict:
    key = jax.random.key(seed)
    ks = jax.random.split(key, 20)
    L = DEPTH
    nrm = jax.random.normal
    f32 = jnp.float32

    def gain(k, n):
        return 1.0 + 0.02 * nrm(k, (L, n), f32)

    return {
        "x": nrm(ks[0], (BATCH, SEQ, D_MODEL), f32),
        "norm_mix_pre": gain(ks[1], D_MODEL),
        "w_in": nrm(ks[2], (L, D_MODEL, D_IN), f32) * D_MODEL ** -0.5,
        "sgu_ln_g": gain(ks[3], A_WIDTH),
        "sgu_ln_b": 0.02 * nrm(ks[4], (L, A_WIDTH), f32),
        "sgu_w_s": nrm(ks[5], (L, A_HEADS, SGU_CHUNK, SGU_CHUNK), f32) * SGU_CHUNK ** -0.5,
        "sgu_b": 1.0 + 0.02 * nrm(ks[6], (L, A_HEADS, SGU_CHUNK), f32),
        "gla_w_gk": nrm(ks[7], (L, GATE_RANK, B_WIDTH_K), f32) * GATE_RANK ** -0.5,
        "gla_b_gk": 0.01 * nrm(ks[8], (L, B_WIDTH_K), f32),
        "gla_norm_g": gain(ks[9], B_HEAD_V),
        "w_out": nrm(ks[10], (L, D_MIX, D_MODEL), f32) * D_MIX ** -0.5,
        "norm_mix_post": gain(ks[11], D_MODEL),
        "norm_ffn_pre": gain(ks[12], D_MODEL),
        "w_up": nrm(ks[13], (L, D_MODEL, 2 * D_FF), f32) * D_MODEL ** -0.5,
        "conv_w": nrm(ks[14], (L, CONV_K, 2 * D_FF), f32) * CONV_K ** -0.5,
        "conv_b": 0.02 * nrm(ks[15], (L, 2 * D_FF), f32),
        "w_down": nrm(ks[16], (L, D_FF, D_MODEL), f32) * D_FF ** -0.5,
        "norm_ffn_post": gain(ks[17], D_MODEL),
    }


def reference(x, norm_mix_pre, w_in, sgu_ln_g, sgu_ln_b, sgu_w_s, sgu_b, gla_w_gk, gla_b_gk,
              gla_norm_g, w_out, norm_mix_post, norm_ffn_pre, w_up, conv_w, conv_b, w_down,
              norm_ffn_post):
    bsz, s, _ = x.shape
    for l in range(DEPTH):
        h = rms_norm(x, norm_mix_pre[l])
        p = h @ w_in[l]
        u = jax.nn.gelu(p[..., OFF_U:OFF_V], approximate=False)
        v_a = jax.nn.gelu(p[..., OFF_V:OFF_Q], approximate=False)
        out_a = sgu_mixer(u, v_a, sgu_ln_g[l], sgu_ln_b[l], sgu_w_s[l], sgu_b[l])
        q = p[..., OFF_Q:OFF_K].reshape(bsz, s, B_HEADS, B_HEAD_K)
        k = p[..., OFF_K:OFF_BV].reshape(bsz, s, B_HEADS, B_HEAD_K)
        v_b = p[..., OFF_BV:OFF_G].reshape(bsz, s, B_HEADS, B_HEAD_V)
        g_out = p[..., OFF_G:OFF_LR].reshape(bsz, s, B_HEADS, B_HEAD_V)
        lr = p[..., OFF_LR:D_IN]
        gk = jax.nn.log_sigmoid((lr @ gla_w_gk[l] + gla_b_gk[l]).astype(jnp.float32)) / GATE_TAU
        gk = gk.reshape(bsz, s, B_HEADS, B_HEAD_K)
        o_b = gla_mixer(q, k, v_b, gk)
        o_b = rms_norm(o_b, gla_norm_g[l]) * jax.nn.silu(g_out.astype(jnp.float32))
        out_b = o_b.reshape(bsz, s, B_WIDTH_V).astype(x.dtype)
        mix = jnp.concatenate([out_a, out_b], axis=-1) @ w_out[l]
        x = x + rms_norm(mix, norm_mix_post[l])
        h = rms_norm(x, norm_ffn_pre[l])
        up = causal_dwconv(h @ w_up[l], conv_w[l], conv_b[l])
        ff = (jax.nn.gelu(up[..., :D_FF], approximate=True) * up[..., D_FF:]) @ w_down[l]
        x = x + rms_norm(ff, norm_ffn_post[l])
    return x
```

```python
import functools
import math

import jax
import jax.numpy as jnp
from jax import lax
from jax.experimental import pallas as pl
from jax.experimental.pallas import tpu as pltpu

F32 = jnp.float32
BF16 = jnp.bfloat16

EPS = 1e-6
A_HEADS = 4
A_HEAD_DIM = 128
A_WIDTH = A_HEADS * A_HEAD_DIM
SGU_CHUNK = 128
B_HEADS = 4
B_HEAD_V = 128
B_HEAD_K = 64
B_WIDTH_V = B_HEADS * B_HEAD_V
B_WIDTH_K = B_HEADS * B_HEAD_K
GATE_RANK = 16
GATE_TAU = 16.0
CONV_K = 3

LANES = 128
GATE_RANK_PAD = LANES
GLA_BLOCK = 128
FF_BLOCK = 256
VMEM_LIMIT = 56 * 1024 * 1024

IN_TILE = 512
MIX_TILE = 256
FFN_TILE = 512


def _rms_norm(x, g):
    ms = jnp.mean(x * x, axis=-1, keepdims=True)
    return x * lax.rsqrt(ms + EPS) * g


def _gelu_exact(x):
    return 0.5 * x * (1.0 + lax.erf(x * (1.0 / math.sqrt(2.0))))


def _gelu_tanh(x):
    c = math.sqrt(2.0 / math.pi)
    return 0.5 * x * (1.0 + jnp.tanh(c * (x + 0.044715 * (x * x * x))))


def _log_sigmoid(x):
    return jnp.minimum(x, 0.0) - jnp.log1p(jnp.exp(-jnp.abs(x)))


def _dot(a, b):
    return jnp.dot(a, b, preferred_element_type=F32)


def _dot_nt(a, b):
    return lax.dot_general(a, b, (((1,), (1,)), ((), ())), preferred_element_type=F32)


def _dot_tn(a, b):
    return lax.dot_general(a, b, (((0,), (0,)), ((), ())), preferred_element_type=F32)


def _in_proj_kernel(x_ref, g_ref, w_ref, wlr_ref, lng_ref, lnb_ref, wgk_ref, bgk_ref,
                    u_ref, v_ref, q_ref, k_ref, vb_ref, sg_ref, gk_ref):
    h = _rms_norm(x_ref[...], g_ref[...]).astype(BF16)

    off = 0
    u_ref[...] = _gelu_exact(_dot(h, w_ref[:, off:off + A_WIDTH])).astype(BF16)
    off += A_WIDTH

    v = _gelu_exact(_dot(h, w_ref[:, off:off + A_WIDTH]))
    mu = jnp.mean(v, axis=-1, keepdims=True)
    vc = v - mu
    var = jnp.mean(vc * vc, axis=-1, keepdims=True)
    v_ref[...] = (vc * lax.rsqrt(var + EPS) * lng_ref[...] + lnb_ref[...]).astype(BF16)
    off += A_WIDTH

    q_ref[...] = _dot(h, w_ref[:, off:off + B_WIDTH_K]).astype(BF16)
    off += B_WIDTH_K
    k_ref[...] = _dot(h, w_ref[:, off:off + B_WIDTH_K]).astype(BF16)
    off += B_WIDTH_K
    vb_ref[...] = _dot(h, w_ref[:, off:off + B_WIDTH_V]).astype(BF16)
    off += B_WIDTH_V
    g_out = _dot(h, w_ref[:, off:off + B_WIDTH_V])
    sg_ref[...] = (g_out * jax.nn.sigmoid(g_out)).astype(BF16)

    lr = _dot(h, wlr_ref[...])
    pre = jnp.dot(lr, wgk_ref[...], preferred_element_type=F32,
                  precision=lax.Precision.HIGHEST) + bgk_ref[...]
    gk_ref[...] = _log_sigmoid(pre) * (1.0 / GATE_TAU)


def _in_proj(x2, g, w_main, w_lr, ln_g, ln_b, w_gk, b_gk):
    t, d = x2.shape
    tm = IN_TILE
    n_main = w_main.shape[1]
    row = lambda i: (i, 0)
    const = lambda i: (0, 0)
    single = pl.Buffered(1)
    return pl.pallas_call(
        _in_proj_kernel,
        grid=(t // tm,),
        in_specs=[
            pl.BlockSpec((tm, d), row),
            pl.BlockSpec((1, d), const),
            pl.BlockSpec((d, n_main), const, pipeline_mode=single),
            pl.BlockSpec((d, GATE_RANK_PAD), const, pipeline_mode=single),
            pl.BlockSpec((1, A_WIDTH), const),
            pl.BlockSpec((1, A_WIDTH), const),
            pl.BlockSpec((GATE_RANK_PAD, B_WIDTH_K), const),
            pl.BlockSpec((1, B_WIDTH_K), const),
        ],
        out_specs=[
            pl.BlockSpec((tm, A_WIDTH), row),
            pl.BlockSpec((tm, A_WIDTH), row),
            pl.BlockSpec((tm, B_WIDTH_K), row),
            pl.BlockSpec((tm, B_WIDTH_K), row),
            pl.BlockSpec((tm, B_WIDTH_V), row),
            pl.BlockSpec((tm, B_WIDTH_V), row),
            pl.BlockSpec((tm, B_WIDTH_K), row),
        ],
        out_shape=[
            jax.ShapeDtypeStruct((t, A_WIDTH), BF16),
            jax.ShapeDtypeStruct((t, A_WIDTH), BF16),
            jax.ShapeDtypeStruct((t, B_WIDTH_K), BF16),
            jax.ShapeDtypeStruct((t, B_WIDTH_K), BF16),
            jax.ShapeDtypeStruct((t, B_WIDTH_V), BF16),
            jax.ShapeDtypeStruct((t, B_WIDTH_V), BF16),
            jax.ShapeDtypeStruct((t, B_WIDTH_K), F32),
        ],
        compiler_params=pltpu.CompilerParams(
            dimension_semantics=("arbitrary",), vmem_limit_bytes=VMEM_LIMIT),
        name="in_proj",
    )(x2, g, w_main, w_lr, ln_g, ln_b, w_gk, b_gk)


def _mixer_kernel(x_ref, u_ref, v_ref, q_ref, k_ref, vb_ref, sg_ref, gk_ref,
                  ws_ref, bias_ref, gng_ref, wout_ref, npost_ref,
                  o_ref, state_ref, mix_ref):
    tm = x_ref.shape[0]

    @pl.when(pl.program_id(1) == 0)
    def _():
        state_ref[...] = jnp.zeros_like(state_ref)

    r = lax.broadcasted_iota(jnp.int32, (SGU_CHUNK, SGU_CHUNK), 0)
    c = lax.broadcasted_iota(jnp.int32, (SGU_CHUNK, SGU_CHUNK), 1)
    causal = r >= c
    for h in range(A_HEADS):
        cols = slice(h * A_HEAD_DIM, (h + 1) * A_HEAD_DIM)
        w_m = jnp.where(causal, ws_ref[h], 0.0).astype(BF16)
        for ch in range(tm // SGU_CHUNK):
            rows = slice(ch * SGU_CHUNK, (ch + 1) * SGU_CHUNK)
            z = _dot(w_m, v_ref[rows, cols]) + bias_ref[:, cols]
            mix_ref[rows, cols] = (u_ref[rows, cols].astype(F32) * z).astype(BF16)

    cb = GLA_BLOCK
    ri = lax.broadcasted_iota(jnp.int32, (cb, cb), 0)
    ci = lax.broadcasted_iota(jnp.int32, (cb, cb), 1)
    tri_incl = ri >= ci
    tri_bf = jnp.where(tri_incl, 1.0, 0.0).astype(BF16)
    lane = lax.broadcasted_iota(jnp.int32, (1, LANES), 1)
    scale = B_HEAD_K ** -0.5
    mid = cb // 2 - 1
    states = [state_ref[h] for h in range(B_HEADS)]
    for blk in range(tm // cb):
        rows = slice(blk * cb, (blk + 1) * cb)
        g = gk_ref[rows, :]
        g_hi = g.astype(BF16)
        g_lo = (g - g_hi.astype(F32)).astype(BF16)
        b = _dot(tri_bf, g_hi) + _dot(tri_bf, g_lo)
        b_mid = b[mid:mid + 1, :]
        b_last = b[cb - 1:cb, :]
        e_pos = jnp.exp(b - b_mid)
        e_neg = jnp.exp(b_mid - b)
        qt = q_ref[rows, :].astype(F32) * scale * e_pos
        kt = k_ref[rows, :].astype(F32) * e_neg
        q_in = (qt * jnp.exp(b_mid)).astype(BF16)
        k_dec = kt * jnp.exp(b_last - b_mid)
        s_decay = jnp.exp(b_last)
        qt = qt.astype(BF16)
        for h in range(B_HEADS):
            tile = slice((h // 2) * LANES, (h // 2 + 1) * LANES)
            in_head = (lane >= (h % 2) * B_HEAD_K) & (lane < (h % 2 + 1) * B_HEAD_K)
            k_m = jnp.where(in_head, kt[:, tile], 0.0).astype(BF16)
            kd_m = jnp.where(in_head, k_dec[:, tile], 0.0).astype(BF16)
            vcols = slice(h * B_HEAD_V, (h + 1) * B_HEAD_V)
            v_h = vb_ref[rows, vcols]
            att = _dot_nt(qt[:, tile], k_m)
            att = jnp.where(tri_incl, att, 0.0).astype(BF16)
            o = _dot(att, v_h) + _dot_nt(q_in[:, tile], states[h].astype(BF16))
            states[h] = states[h] * s_decay[:, tile] + _dot_tn(v_h, kd_m)
            o = _rms_norm(o, gng_ref[...]) * sg_ref[rows, vcols].astype(F32)
            mix_ref[rows, A_WIDTH + h * B_HEAD_V:A_WIDTH + (h + 1) * B_HEAD_V] = o.astype(BF16)
    for h in range(B_HEADS):
        state_ref[h] = states[h]

    mix = _dot(mix_ref[...], wout_ref[...])
    o_ref[...] = x_ref[...] + _rms_norm(mix, npost_ref[...])


def _mixer(x2, u, v, q, k, vb, sg, gk, w_s, bias_full, gn_g, w_out, n_post, batch):
    t, d = x2.shape
    tm = MIX_TILE
    per_b = t // batch // tm
    row = lambda b, i: (b * per_b + i, 0)
    const2 = lambda b, i: (0, 0)
    const3 = lambda b, i: (0, 0, 0)
    single = pl.Buffered(1)
    return pl.pallas_call(
        _mixer_kernel,
        grid=(batch, per_b),
        in_specs=[
            pl.BlockSpec((tm, d), row),
            pl.BlockSpec((tm, A_WIDTH), row),
            pl.BlockSpec((tm, A_WIDTH), row),
            pl.BlockSpec((tm, B_WIDTH_K), row),
            pl.BlockSpec((tm, B_WIDTH_K), row),
            pl.BlockSpec((tm, B_WIDTH_V), row),
            pl.BlockSpec((tm, B_WIDTH_V), row),
            pl.BlockSpec((tm, B_WIDTH_K), row),
            pl.BlockSpec((A_HEADS, SGU_CHUNK, SGU_CHUNK), const3),
            pl.BlockSpec((SGU_CHUNK, A_WIDTH), const2),
            pl.BlockSpec((1, B_HEAD_V), const2),
            pl.BlockSpec((d, d), const2, pipeline_mode=single),
            pl.BlockSpec((1, d), const2),
        ],
        out_specs=pl.BlockSpec((tm, d), row),
        out_shape=jax.ShapeDtypeStruct((t, d), F32),
        scratch_shapes=[
            pltpu.VMEM((B_HEADS, B_HEAD_V, LANES), F32),
            pltpu.VMEM((tm, d), BF16),
        ],
        compiler_params=pltpu.CompilerParams(
            dimension_semantics=("arbitrary", "arbitrary"), vmem_limit_bytes=VMEM_LIMIT),
        name="mixer",
    )(x2, u, v, q, k, vb, sg, gk, w_s, bias_full, gn_g, w_out, n_post)


def _shift_rows(y, prev2, n):
    rolled = pltpu.roll(y, n, 0)
    row8 = lax.broadcasted_iota(jnp.int32, (8, y.shape[1]), 0)
    head = rolled[:8]
    if n == 1:
        head = jnp.where(row8 == 0, prev2[1:2], head)
    else:
        head = jnp.where(row8 == 0, prev2[0:1], jnp.where(row8 == 1, prev2[1:2], head))
    return jnp.concatenate([head, rolled[8:]], axis=0)


def _conv_ffn_kernel(x_ref, gpre_ref, wup_ref, cw_ref, cb_ref, wdn_ref, gpost_ref,
                     o_ref, carry_ref):
    tm = x_ref.shape[0]
    d_ff = wdn_ref.shape[0]

    @pl.when(pl.program_id(1) == 0)
    def _():
        carry_ref[...] = jnp.zeros_like(carry_ref)

    x = x_ref[...]
    h = _rms_norm(x, gpre_ref[...]).astype(BF16)

    def conv(cols):
        y = _dot(h, wup_ref[:, cols])
        prev2 = carry_ref[:, cols]
        carry_ref[:, cols] = y[tm - 2:tm]
        w = cw_ref[:, cols]
        return (cb_ref[:, cols] + _shift_rows(y, prev2, 2) * w[0:1]
                + _shift_rows(y, prev2, 1) * w[1:2] + y * w[2:3])

    acc = jnp.zeros((tm, o_ref.shape[1]), F32)
    for j in range(d_ff // FF_BLOCK):
        ca = slice(j * FF_BLOCK, (j + 1) * FF_BLOCK)
        cg = slice(d_ff + j * FF_BLOCK, d_ff + (j + 1) * FF_BLOCK)
        act = (_gelu_tanh(conv(ca)) * conv(cg)).astype(BF16)
        acc = acc + _dot(act, wdn_ref[ca, :])
    o_ref[...] = x + _rms_norm(acc, gpost_ref[...])


def _conv_ffn(x2, g_pre, w_up, conv_w, conv_b, w_down, g_post, batch):
    t, d = x2.shape
    tm = FFN_TILE
    per_b = t // batch // tm
    d_ff = w_down.shape[0]
    row = lambda b, i: (b * per_b + i, 0)
    const = lambda b, i: (0, 0)
    single = pl.Buffered(1)
    return pl.pallas_call(
        _conv_ffn_kernel,
        grid=(batch, per_b),
        in_specs=[
            pl.BlockSpec((tm, d), row),
            pl.BlockSpec((1, d), const),
            pl.BlockSpec((d, 2 * d_ff), const, pipeline_mode=single),
            pl.BlockSpec((CONV_K, 2 * d_ff), const),
            pl.BlockSpec((1, 2 * d_ff), const),
            pl.BlockSpec((d_ff, d), const, pipeline_mode=single),
            pl.BlockSpec((1, d), const),
        ],
        out_specs=pl.BlockSpec((tm, d), row),
        out_shape=jax.ShapeDtypeStruct((t, d), F32),
        scratch_shapes=[pltpu.VMEM((CONV_K - 1, 2 * d_ff), F32)],
        compiler_params=pltpu.CompilerParams(
            dimension_semantics=("arbitrary", "arbitrary"), vmem_limit_bytes=VMEM_LIMIT),
        name="conv_ffn",
    )(x2, g_pre, w_up, conv_w, conv_b, w_down, g_post)


def kernel(x, norm_mix_pre, w_in, sgu_ln_g, sgu_ln_b, sgu_w_s, sgu_b, gla_w_gk, gla_b_gk,
           gla_norm_g, w_out, norm_mix_post, norm_ffn_pre, w_up, conv_w, conv_b, w_down,
           norm_ffn_post):
    bsz, s, d = x.shape
    depth = w_in.shape[0]
    n_main = w_in.shape[2] - GATE_RANK
    x2 = x.reshape(bsz * s, d)
    for l in range(depth):
        w_main = w_in[l, :, :n_main].astype(BF16)
        w_lr = jnp.pad(w_in[l, :, n_main:], ((0, 0), (0, GATE_RANK_PAD - GATE_RANK))).astype(BF16)
        w_gk = jnp.pad(gla_w_gk[l], ((0, GATE_RANK_PAD - GATE_RANK), (0, 0)))
        u, v, q, k, vb, sg, gk = _in_proj(
            x2, norm_mix_pre[l][None], w_main, w_lr, sgu_ln_g[l][None], sgu_ln_b[l][None],
            w_gk, gla_b_gk[l][None])
        bias_full = jnp.repeat(jnp.transpose(sgu_b[l]), A_HEAD_DIM, axis=1)
        x2 = _mixer(x2, u, v, q, k, vb, sg, gk, sgu_w_s[l], bias_full, gla_norm_g[l][None],
                    w_out[l].astype(BF16), norm_mix_post[l][None], bsz)
        x2 = _conv_ffn(x2, norm_ffn_pre[l][None], w_up[l].astype(BF16), conv_w[l],
                       conv_b[l][None], w_down[l].astype(BF16), norm_ffn_post[l][None], bsz)
    return x2.reshape(bsz, s, d)
```

```python
import functools
import math

import jax
import jax.numpy as jnp
from jax import lax
from jax.experimental import pallas as pl
from jax.experimental.pallas import tpu as pltpu

F32 = jnp.float32
BF16 = jnp.bfloat16

EPS = 1e-6
A_HEADS = 4
A_HEAD_DIM = 128
A_WIDTH = A_HEADS * A_HEAD_DIM
SGU_CHUNK = 128
B_HEADS = 4
B_HEAD_V = 128
B_HEAD_K = 64
B_WIDTH_V = B_HEADS * B_HEAD_V
B_WIDTH_K = B_HEADS * B_HEAD_K
GATE_RANK = 16
GATE_TAU = 16.0
CONV_K = 3

LANES = 128
SUBLANES = 8
GATE_RANK_PAD = LANES
GLA_BLOCK = 128
FF_BLOCK = 256
VMEM_LIMIT = 56 * 1024 * 1024

IN_TILE = 512
MIX_TILE = 256
FFN_TILE = 512


def _rms_norm(x, g):
    ms = jnp.mean(x * x, axis=-1, keepdims=True)
    return x * lax.rsqrt(ms + EPS) * g


def _gelu_exact(x):
    return 0.5 * x * (1.0 + lax.erf(x * (1.0 / math.sqrt(2.0))))


def _gelu_tanh(x):
    c = math.sqrt(2.0 / math.pi)
    return 0.5 * x * (1.0 + jnp.tanh(c * (x + 0.044715 * (x * x * x))))


def _log_sigmoid(x):
    return jnp.minimum(x, 0.0) - jnp.log1p(jnp.exp(-jnp.abs(x)))


def _dot(a, b):
    return jnp.dot(a, b, preferred_element_type=F32)


def _dot_nt(a, b):
    return lax.dot_general(a, b, (((1,), (1,)), ((), ())), preferred_element_type=F32)


def _dot_tn(a, b):
    return lax.dot_general(a, b, (((0,), (0,)), ((), ())), preferred_element_type=F32)


def _in_proj_kernel(x_ref, g_ref, w_ref, wlr_ref, lng_ref, lnb_ref, wgk_ref, bgk_ref,
                    u_ref, v_ref, q_ref, k_ref, vb_ref, sg_ref, gk_ref):
    h = _rms_norm(x_ref[...], g_ref[...]).astype(BF16)

    off = 0
    u_ref[...] = _gelu_exact(_dot(h, w_ref[:, off:off + A_WIDTH])).astype(BF16)
    off += A_WIDTH

    v = _gelu_exact(_dot(h, w_ref[:, off:off + A_WIDTH]))
    mu = jnp.mean(v, axis=-1, keepdims=True)
    vc = v - mu
    var = jnp.mean(vc * vc, axis=-1, keepdims=True)
    v_ref[...] = (vc * lax.rsqrt(var + EPS) * lng_ref[...] + lnb_ref[...]).astype(BF16)
    off += A_WIDTH

    q_ref[...] = _dot(h, w_ref[:, off:off + B_WIDTH_K]).astype(BF16)
    off += B_WIDTH_K
    k_ref[...] = _dot(h, w_ref[:, off:off + B_WIDTH_K]).astype(BF16)
    off += B_WIDTH_K
    vb_ref[...] = _dot(h, w_ref[:, off:off + B_WIDTH_V]).astype(BF16)
    off += B_WIDTH_V
    g_out = _dot(h, w_ref[:, off:off + B_WIDTH_V])
    sg_ref[...] = (g_out * jax.nn.sigmoid(g_out)).astype(BF16)

    lr = _dot(h, wlr_ref[...])
    pre = jnp.dot(lr, wgk_ref[...], preferred_element_type=F32,
                  precision=lax.Precision.HIGHEST) + bgk_ref[...]
    gk_ref[...] = _log_sigmoid(pre) * (1.0 / GATE_TAU)


def _in_proj(x2, g, w_main, w_lr, ln_g, ln_b, w_gk, b_gk):
    t, d = x2.shape
    tm = IN_TILE
    n_main = w_main.shape[1]
    row = lambda i: (i, 0)
    const = lambda i: (0, 0)
    single = pl.Buffered(1)
    return pl.pallas_call(
        _in_proj_kernel,
        grid=(t // tm,),
        in_specs=[
            pl.BlockSpec((tm, d), row),
            pl.BlockSpec((1, d), const),
            pl.BlockSpec((d, n_main), const, pipeline_mode=single),
            pl.BlockSpec((d, GATE_RANK_PAD), const, pipeline_mode=single),
            pl.BlockSpec((1, A_WIDTH), const),
            pl.BlockSpec((1, A_WIDTH), const),
            pl.BlockSpec((GATE_RANK_PAD, B_WIDTH_K), const),
            pl.BlockSpec((1, B_WIDTH_K), const),
        ],
        out_specs=[
            pl.BlockSpec((tm, A_WIDTH), row),
            pl.BlockSpec((tm, A_WIDTH), row),
            pl.BlockSpec((tm, B_WIDTH_K), row),
            pl.BlockSpec((tm, B_WIDTH_K), row),
            pl.BlockSpec((tm, B_WIDTH_V), row),
            pl.BlockSpec((tm, B_WIDTH_V), row),
            pl.BlockSpec((tm, B_WIDTH_K), row),
        ],
        out_shape=[
            jax.ShapeDtypeStruct((t, A_WIDTH), BF16),
            jax.ShapeDtypeStruct((t, A_WIDTH), BF16),
            jax.ShapeDtypeStruct((t, B_WIDTH_K), BF16),
            jax.ShapeDtypeStruct((t, B_WIDTH_K), BF16),
            jax.ShapeDtypeStruct((t, B_WIDTH_V), BF16),
            jax.ShapeDtypeStruct((t, B_WIDTH_V), BF16),
            jax.ShapeDtypeStruct((t, B_WIDTH_K), F32),
        ],
        compiler_params=pltpu.CompilerParams(
            dimension_semantics=("arbitrary",), vmem_limit_bytes=VMEM_LIMIT),
        name="in_proj",
    )(x2, g, w_main, w_lr, ln_g, ln_b, w_gk, b_gk)


def _mixer_kernel(x_ref, u_ref, v_ref, q_ref, k_ref, vb_ref, sg_ref, gk_ref,
                  ws_ref, bias_ref, gng_ref, wout_ref, npost_ref,
                  o_ref, state_ref, mix_ref):
    tm = x_ref.shape[0]

    @pl.when(pl.program_id(1) == 0)
    def _():
        state_ref[...] = jnp.zeros_like(state_ref)

    r = lax.broadcasted_iota(jnp.int32, (SGU_CHUNK, SGU_CHUNK), 0)
    c = lax.broadcasted_iota(jnp.int32, (SGU_CHUNK, SGU_CHUNK), 1)
    causal = r >= c
    for h in range(A_HEADS):
        cols = slice(h * A_HEAD_DIM, (h + 1) * A_HEAD_DIM)
        w_m = jnp.where(causal, ws_ref[h], 0.0).astype(BF16)
        for ch in range(tm // SGU_CHUNK):
            rows = slice(ch * SGU_CHUNK, (ch + 1) * SGU_CHUNK)
            z = _dot(w_m, v_ref[rows, cols]) + bias_ref[:, cols]
            mix_ref[rows, cols] = (u_ref[rows, cols].astype(F32) * z).astype(BF16)

    cb = GLA_BLOCK
    ri = lax.broadcasted_iota(jnp.int32, (cb, cb), 0)
    ci = lax.broadcasted_iota(jnp.int32, (cb, cb), 1)
    tri_incl = ri >= ci
    tri_bf = jnp.where(tri_incl, 1.0, 0.0).astype(BF16)
    lane = lax.broadcasted_iota(jnp.int32, (1, LANES), 1)
    scale = B_HEAD_K ** -0.5
    mid = cb // 2 - 1
    states = [state_ref[h] for h in range(B_HEADS)]
    for blk in range(tm // cb):
        rows = slice(blk * cb, (blk + 1) * cb)
        g = gk_ref[rows, :]
        g_hi = g.astype(BF16)
        g_lo = (g - g_hi.astype(F32)).astype(BF16)
        b = _dot(tri_bf, g_hi) + _dot(tri_bf, g_lo)
        b_mid = b[mid:mid + 1, :]
        b_last = b[cb - 1:cb, :]
        e_pos = jnp.exp(b - b_mid)
        e_neg = jnp.exp(b_mid - b)
        qt = q_ref[rows, :].astype(F32) * scale * e_pos
        kt = k_ref[rows, :].astype(F32) * e_neg
        q_in = (qt * jnp.exp(b_mid)).astype(BF16)
        k_dec = kt * jnp.exp(b_last - b_mid)
        s_decay = jnp.exp(b_last)
        qt = qt.astype(BF16)
        for h in range(B_HEADS):
            tile = slice((h // 2) * LANES, (h // 2 + 1) * LANES)
            in_head = (lane >= (h % 2) * B_HEAD_K) & (lane < (h % 2 + 1) * B_HEAD_K)
            k_m = jnp.where(in_head, kt[:, tile], 0.0).astype(BF16)
            kd_m = jnp.where(in_head, k_dec[:, tile], 0.0).astype(BF16)
            vcols = slice(h * B_HEAD_V, (h + 1) * B_HEAD_V)
            v_h = vb_ref[rows, vcols]
            att = _dot_nt(qt[:, tile], k_m)
            att = jnp.where(tri_incl, att, 0.0).astype(BF16)
            o = _dot(att, v_h) + _dot_nt(q_in[:, tile], states[h].astype(BF16))
            states[h] = states[h] * s_decay[:, tile] + _dot_tn(v_h, kd_m)
            o = _rms_norm(o, gng_ref[...]) * sg_ref[rows, vcols].astype(F32)
            mix_ref[rows, A_WIDTH + h * B_HEAD_V:A_WIDTH + (h + 1) * B_HEAD_V] = o.astype(BF16)
    for h in range(B_HEADS):
        state_ref[h] = states[h]

    mix = _dot(mix_ref[...], wout_ref[...])
    o_ref[...] = x_ref[...] + _rms_norm(mix, npost_ref[...])


def _mixer(x2, u, v, q, k, vb, sg, gk, w_s, bias_full, gn_g, w_out, n_post, batch):
    t, d = x2.shape
    tm = MIX_TILE
    per_b = t // batch // tm
    row = lambda b, i: (b * per_b + i, 0)
    const2 = lambda b, i: (0, 0)
    const3 = lambda b, i: (0, 0, 0)
    single = pl.Buffered(1)
    return pl.pallas_call(
        _mixer_kernel,
        grid=(batch, per_b),
        in_specs=[
            pl.BlockSpec((tm, d), row),
            pl.BlockSpec((tm, A_WIDTH), row),
            pl.BlockSpec((tm, A_WIDTH), row),
            pl.BlockSpec((tm, B_WIDTH_K), row),
            pl.BlockSpec((tm, B_WIDTH_K), row),
            pl.BlockSpec((tm, B_WIDTH_V), row),
            pl.BlockSpec((tm, B_WIDTH_V), row),
            pl.BlockSpec((tm, B_WIDTH_K), row),
            pl.BlockSpec((A_HEADS, SGU_CHUNK, SGU_CHUNK), const3),
            pl.BlockSpec((SGU_CHUNK, A_WIDTH), const2),
            pl.BlockSpec((1, B_HEAD_V), const2),
            pl.BlockSpec((d, d), const2, pipeline_mode=single),
            pl.BlockSpec((1, d), const2),
        ],
        out_specs=pl.BlockSpec((tm, d), row),
        out_shape=jax.ShapeDtypeStruct((t, d), F32),
        scratch_shapes=[
            pltpu.VMEM((B_HEADS, B_HEAD_V, LANES), F32),
            pltpu.VMEM((tm, d), BF16),
        ],
        compiler_params=pltpu.CompilerParams(
            dimension_semantics=("arbitrary", "arbitrary"), vmem_limit_bytes=VMEM_LIMIT),
        name="mixer",
    )(x2, u, v, q, k, vb, sg, gk, w_s, bias_full, gn_g, w_out, n_post)


def _perm_base(r, n_groups):
    per = n_groups // SUBLANES
    return (r % per) * SUBLANES * SUBLANES + r // per


def _conv_ffn_kernel(x_ref, gpre_ref, wup_ref, cw_ref, cb_ref, wdn_ref, gpost_ref,
                     o_ref, carry_ref, hperm_ref, rperm_ref, y_ref):
    tm, d = x_ref.shape
    d_ff = wdn_ref.shape[0]
    ng = tm // SUBLANES
    nslab = d // LANES
    nblk = d_ff // FF_BLOCK

    @pl.when(pl.program_id(1) == 0)
    def _():
        carry_ref[...] = jnp.zeros_like(carry_ref)

    h32 = _rms_norm(x_ref[...], gpre_ref[...])
    for r in range(ng):
        base = _perm_base(r, ng)
        for sl in range(nslab):
            hperm_ref[sl, pl.ds(base, SUBLANES, stride=SUBLANES), :] = (
                h32[r * SUBLANES:(r + 1) * SUBLANES, sl * LANES:(sl + 1) * LANES])
    h = jnp.concatenate([hperm_ref[sl] for sl in range(nslab)], axis=-1).astype(BF16)

    sub = lax.broadcasted_iota(jnp.int32, (SUBLANES, FF_BLOCK), 0)

    def up(j):
        ca = slice(j * FF_BLOCK, (j + 1) * FF_BLOCK)
        cg = slice(d_ff + j * FF_BLOCK, d_ff + (j + 1) * FF_BLOCK)
        y_ref[j % 2, 0] = _dot(h, wup_ref[:, ca])
        y_ref[j % 2, 1] = _dot(h, wup_ref[:, cg])

    def conv(j, part):
        cols = slice(part * d_ff + j * FF_BLOCK, part * d_ff + (j + 1) * FF_BLOCK)
        yr = y_ref.at[j % 2, part]
        y = yr[...]
        prev = carry_ref[:, cols]
        last2 = yr[tm - 2 * SUBLANES:tm, :]
        carry_ref[0:1, cols] = last2[SUBLANES - 1:SUBLANES]
        carry_ref[1:2, cols] = last2[2 * SUBLANES - 1:2 * SUBLANES]
        f2 = jnp.where(sub == 0, prev[0:1], pltpu.roll(last2[:SUBLANES], 1, 0))
        f1 = jnp.where(sub == 0, prev[1:2], pltpu.roll(last2[SUBLANES:], 1, 0))
        y1 = jnp.concatenate([f1, y[:tm - SUBLANES]], axis=0)
        y2 = jnp.concatenate([f2, f1, y[:tm - 2 * SUBLANES]], axis=0)
        w = cw_ref[:, cols]
        return cb_ref[:, cols] + y2 * w[0:1] + y1 * w[1:2] + y * w[2:3]

    acc = jnp.zeros((tm, d), F32)
    up(0)
    for j in range(nblk):
        if j + 1 < nblk:
            up(j + 1)
        act = (_gelu_tanh(conv(j, 0)) * conv(j, 1)).astype(BF16)
        acc = acc + _dot(act, wdn_ref[j * FF_BLOCK:(j + 1) * FF_BLOCK, :])

    res = _rms_norm(acc, gpost_ref[...])
    for sl in range(nslab):
        rperm_ref[sl] = res[:, sl * LANES:(sl + 1) * LANES]
    for r in range(ng):
        base = _perm_base(r, ng)
        for sl in range(nslab):
            rows = slice(r * SUBLANES, (r + 1) * SUBLANES)
            cols = slice(sl * LANES, (sl + 1) * LANES)
            o_ref[rows, cols] = (x_ref[rows, cols]
                                 + rperm_ref[sl, pl.ds(base, SUBLANES, stride=SUBLANES), :])


def _conv_ffn(x2, g_pre, w_up, conv_w, conv_b, w_down, g_post, batch):
    t, d = x2.shape
    tm = FFN_TILE
    per_b = t // batch // tm
    d_ff = w_down.shape[0]
    row = lambda b, i: (b * per_b + i, 0)
    const = lambda b, i: (0, 0)
    single = pl.Buffered(1)
    return pl.pallas_call(
        _conv_ffn_kernel,
        grid=(batch, per_b),
        in_specs=[
            pl.BlockSpec((tm, d), row),
            pl.BlockSpec((1, d), const),
            pl.BlockSpec((d, 2 * d_ff), const, pipeline_mode=single),
            pl.BlockSpec((CONV_K, 2 * d_ff), const),
            pl.BlockSpec((1, 2 * d_ff), const),
            pl.BlockSpec((d_ff, d), const, pipeline_mode=single),
            pl.BlockSpec((1, d), const),
        ],
        out_specs=pl.BlockSpec((tm, d), row),
        out_shape=jax.ShapeDtypeStruct((t, d), F32),
        scratch_shapes=[
            pltpu.VMEM((CONV_K - 1, 2 * d_ff), F32),
            pltpu.VMEM((d // LANES, tm, LANES), F32),
            pltpu.VMEM((d // LANES, tm, LANES), F32),
            pltpu.VMEM((2, 2, tm, FF_BLOCK), F32),
        ],
        compiler_params=pltpu.CompilerParams(
            dimension_semantics=("arbitrary", "arbitrary"), vmem_limit_bytes=VMEM_LIMIT),
        name="conv_ffn",
    )(x2, g_pre, w_up, conv_w, conv_b, w_down, g_post)


def kernel(x, norm_mix_pre, w_in, sgu_ln_g, sgu_ln_b, sgu_w_s, sgu_b, gla_w_gk, gla_b_gk,
           gla_norm_g, w_out, norm_mix_post, norm_ffn_pre, w_up, conv_w, conv_b, w_down,
           norm_ffn_post):
    bsz, s, d = x.shape
    depth = w_in.shape[0]
    n_main = w_in.shape[2] - GATE_RANK
    x2 = x.reshape(bsz * s, d)
    for l in range(depth):
        w_main = w_in[l, :, :n_main].astype(BF16)
        w_lr = jnp.pad(w_in[l, :, n_main:], ((0, 0), (0, GATE_RANK_PAD - GATE_RANK))).astype(BF16)
        w_gk = jnp.pad(gla_w_gk[l], ((0, GATE_RANK_PAD - GATE_RANK), (0, 0)))
        u, v, q, k, vb, sg, gk = _in_proj(
            x2, norm_mix_pre[l][None], w_main, w_lr, sgu_ln_g[l][None], sgu_ln_b[l][None],
            w_gk, gla_b_gk[l][None])
        bias_full = jnp.repeat(jnp.transpose(sgu_b[l]), A_HEAD_DIM, axis=1)
        x2 = _mixer(x2, u, v, q, k, vb, sg, gk, sgu_w_s[l], bias_full, gla_norm_g[l][None],
                    w_out[l].astype(BF16), norm_mix_post[l][None], bsz)
        x2 = _conv_ffn(x2, norm_ffn_pre[l][None], w_up[l].astype(BF16), conv_w[l],
                       conv_b[l][None], w_down[l].astype(BF16), norm_ffn_post[l][None], bsz)
    return x2.reshape(bsz, s, d)
```

```python
import functools
import math

import jax
import jax.numpy as jnp
from jax import lax
from jax.experimental import pallas as pl
from jax.experimental.pallas import tpu as pltpu

F32 = jnp.float32
BF16 = jnp.bfloat16

EPS = 1e-6
A_HEADS = 4
A_HEAD_DIM = 128
A_WIDTH = A_HEADS * A_HEAD_DIM
SGU_CHUNK = 128
B_HEADS = 4
B_HEAD_V = 128
B_HEAD_K = 64
B_WIDTH_V = B_HEADS * B_HEAD_V
B_WIDTH_K = B_HEADS * B_HEAD_K
GATE_RANK = 16
GATE_TAU = 16.0
CONV_K = 3

LANES = 128
SUBLANES = 8
GATE_RANK_PAD = LANES
GLA_BLOCK = 128
FF_BLOCK = 256
VMEM_LIMIT = 56 * 1024 * 1024

IN_TILE = 1024
MIX_TILE = 1024
FFN_TILE = 512


def _rms_norm(x, g):
    ms = jnp.mean(x * x, axis=-1, keepdims=True)
    return x * lax.rsqrt(ms + EPS) * g


def _gelu_exact(x):
    return 0.5 * x * (1.0 + lax.erf(x * (1.0 / math.sqrt(2.0))))


def _gelu_tanh(x):
    c = math.sqrt(2.0 / math.pi)
    return 0.5 * x * (1.0 + jnp.tanh(c * (x + 0.044715 * (x * x * x))))


def _log_sigmoid(x):
    return jnp.minimum(x, 0.0) - jnp.log1p(jnp.exp(-jnp.abs(x)))


def _dot(a, b):
    return jnp.dot(a, b, preferred_element_type=F32)


def _dot_nt(a, b):
    return lax.dot_general(a, b, (((1,), (1,)), ((), ())), preferred_element_type=F32)


def _dot_tn(a, b):
    return lax.dot_general(a, b, (((0,), (0,)), ((), ())), preferred_element_type=F32)


def _in_proj_kernel(x_ref, g_ref, w_ref, wlr_ref, lng_ref, lnb_ref, wgk_ref, bgk_ref,
                    u_ref, v_ref, q_ref, k_ref, vb_ref, sg_ref, gk_ref, wg_ref):
    @pl.when(pl.program_id(0) == 0)
    def _():
        wg_ref[...] = jnp.dot(wlr_ref[...], wgk_ref[...], preferred_element_type=F32,
                              precision=lax.Precision.HIGHEST).astype(BF16)

    h = _rms_norm(x_ref[...], g_ref[...]).astype(BF16)
    off_u = 0
    off_v = off_u + A_WIDTH
    off_q = off_v + A_WIDTH
    off_k = off_q + B_WIDTH_K
    off_vb = off_k + B_WIDTH_K
    off_g = off_vb + B_WIDTH_V

    u_ref[...] = _gelu_exact(_dot(h, w_ref[:, off_u:off_u + A_WIDTH])).astype(BF16)

    v = _gelu_exact(_dot(h, w_ref[:, off_v:off_v + A_WIDTH]))
    mu = jnp.mean(v, axis=-1, keepdims=True)
    vc = v - mu
    var = jnp.mean(vc * vc, axis=-1, keepdims=True)
    v_ref[...] = (vc * lax.rsqrt(var + EPS) * lng_ref[...] + lnb_ref[...]).astype(BF16)

    g_out = _dot(h, w_ref[:, off_g:off_g + B_WIDTH_V])
    sg_ref[...] = (g_out * jax.nn.sigmoid(g_out)).astype(BF16)

    pre = _dot(h, wg_ref[...]) + bgk_ref[...]
    gk_ref[...] = _log_sigmoid(pre) * (1.0 / GATE_TAU)

    vb_ref[...] = _dot(h, w_ref[:, off_vb:off_vb + B_WIDTH_V]).astype(BF16)
    q_ref[...] = _dot(h, w_ref[:, off_q:off_q + B_WIDTH_K]).astype(BF16)
    k_ref[...] = _dot(h, w_ref[:, off_k:off_k + B_WIDTH_K]).astype(BF16)


def _in_proj(x2, g, w_main, w_lr, ln_g, ln_b, w_gk, b_gk):
    t, d = x2.shape
    tm = IN_TILE
    n_main = w_main.shape[1]
    row = lambda i: (i, 0)
    const = lambda i: (0, 0)
    single = pl.Buffered(1)
    return pl.pallas_call(
        _in_proj_kernel,
        grid=(t // tm,),
        in_specs=[
            pl.BlockSpec((tm, d), row),
            pl.BlockSpec((1, d), const),
            pl.BlockSpec((d, n_main), const, pipeline_mode=single),
            pl.BlockSpec((d, GATE_RANK_PAD), const, pipeline_mode=single),
            pl.BlockSpec((1, A_WIDTH), const),
            pl.BlockSpec((1, A_WIDTH), const),
            pl.BlockSpec((GATE_RANK_PAD, B_WIDTH_K), const),
            pl.BlockSpec((1, B_WIDTH_K), const),
        ],
        out_specs=[
            pl.BlockSpec((tm, A_WIDTH), row),
            pl.BlockSpec((tm, A_WIDTH), row),
            pl.BlockSpec((tm, B_WIDTH_K), row),
            pl.BlockSpec((tm, B_WIDTH_K), row),
            pl.BlockSpec((tm, B_WIDTH_V), row),
            pl.BlockSpec((tm, B_WIDTH_V), row),
            pl.BlockSpec((tm, B_WIDTH_K), row),
        ],
        out_shape=[
            jax.ShapeDtypeStruct((t, A_WIDTH), BF16),
            jax.ShapeDtypeStruct((t, A_WIDTH), BF16),
            jax.ShapeDtypeStruct((t, B_WIDTH_K), BF16),
            jax.ShapeDtypeStruct((t, B_WIDTH_K), BF16),
            jax.ShapeDtypeStruct((t, B_WIDTH_V), BF16),
            jax.ShapeDtypeStruct((t, B_WIDTH_V), BF16),
            jax.ShapeDtypeStruct((t, B_WIDTH_K), F32),
        ],
        scratch_shapes=[pltpu.VMEM((d, B_WIDTH_K), BF16)],
        compiler_params=pltpu.CompilerParams(
            dimension_semantics=("arbitrary",), vmem_limit_bytes=VMEM_LIMIT),
        name="in_proj",
    )(x2, g, w_main, w_lr, ln_g, ln_b, w_gk, b_gk)


def _mixer_kernel(x_ref, u_ref, v_ref, q_ref, k_ref, vb_ref, sg_ref, gk_ref,
                  ws_ref, bias_ref, gng_ref, wout_ref, npost_ref,
                  o_ref, state_ref, mix_ref):
    tm = x_ref.shape[0]

    @pl.when(pl.program_id(1) == 0)
    def _():
        state_ref[...] = jnp.zeros_like(state_ref)

    r = lax.broadcasted_iota(jnp.int32, (SGU_CHUNK, SGU_CHUNK), 0)
    c = lax.broadcasted_iota(jnp.int32, (SGU_CHUNK, SGU_CHUNK), 1)
    causal = r >= c
    for h in range(A_HEADS):
        cols = slice(h * A_HEAD_DIM, (h + 1) * A_HEAD_DIM)
        w_m = jnp.where(causal, ws_ref[h], 0.0).astype(BF16)
        for ch in range(tm // SGU_CHUNK):
            rows = slice(ch * SGU_CHUNK, (ch + 1) * SGU_CHUNK)
            z = _dot(w_m, v_ref[rows, cols]) + bias_ref[:, cols]
            mix_ref[rows, cols] = (u_ref[rows, cols].astype(F32) * z).astype(BF16)

    cb = GLA_BLOCK
    ri = lax.broadcasted_iota(jnp.int32, (cb, cb), 0)
    ci = lax.broadcasted_iota(jnp.int32, (cb, cb), 1)
    tri_incl = ri >= ci
    tri_bf = jnp.where(tri_incl, 1.0, 0.0).astype(BF16)
    lane = lax.broadcasted_iota(jnp.int32, (1, LANES), 1)
    scale = B_HEAD_K ** -0.5
    mid = cb // 2 - 1
    states = [state_ref[h] for h in range(B_HEADS)]
    for blk in range(tm // cb):
        rows = slice(blk * cb, (blk + 1) * cb)
        g = gk_ref[rows, :]
        g_hi = g.astype(BF16)
        g_lo = (g - g_hi.astype(F32)).astype(BF16)
        b = _dot(tri_bf, g_hi) + _dot(tri_bf, g_lo)
        b_mid = b[mid:mid + 1, :]
        b_last = b[cb - 1:cb, :]
        e_pos = jnp.exp(b - b_mid)
        e_neg = jnp.exp(b_mid - b)
        qt = q_ref[rows, :].astype(F32) * scale * e_pos
        kt = k_ref[rows, :].astype(F32) * e_neg
        q_in = (qt * jnp.exp(b_mid)).astype(BF16)
        k_dec = kt * jnp.exp(b_last - b_mid)
        s_decay = jnp.exp(b_last)
        qt = qt.astype(BF16)
        for h in range(B_HEADS):
            tile = slice((h // 2) * LANES, (h // 2 + 1) * LANES)
            in_head = (lane >= (h % 2) * B_HEAD_K) & (lane < (h % 2 + 1) * B_HEAD_K)
            k_m = jnp.where(in_head, kt[:, tile], 0.0).astype(BF16)
            kd_m = jnp.where(in_head, k_dec[:, tile], 0.0).astype(BF16)
            vcols = slice(h * B_HEAD_V, (h + 1) * B_HEAD_V)
            v_h = vb_ref[rows, vcols]
            att = _dot_nt(qt[:, tile], k_m)
            att = jnp.where(tri_incl, att, 0.0).astype(BF16)
            o = _dot(att, v_h) + _dot_nt(q_in[:, tile], states[h].astype(BF16))
            states[h] = states[h] * s_decay[:, tile] + _dot_tn(v_h, kd_m)
            o = _rms_norm(o, gng_ref[...]) * sg_ref[rows, vcols].astype(F32)
            mix_ref[rows, A_WIDTH + h * B_HEAD_V:A_WIDTH + (h + 1) * B_HEAD_V] = o.astype(BF16)
    for h in range(B_HEADS):
        state_ref[h] = states[h]

    mix = _dot(mix_ref[...], wout_ref[...])
    o_ref[...] = x_ref[...] + _rms_norm(mix, npost_ref[...])


def _mixer(x2, u, v, q, k, vb, sg, gk, w_s, bias_full, gn_g, w_out, n_post, batch):
    t, d = x2.shape
    tm = MIX_TILE
    per_b = t // batch // tm
    row = lambda b, i: (b * per_b + i, 0)
    const2 = lambda b, i: (0, 0)
    const3 = lambda b, i: (0, 0, 0)
    single = pl.Buffered(1)
    return pl.pallas_call(
        _mixer_kernel,
        grid=(batch, per_b),
        in_specs=[
            pl.BlockSpec((tm, d), row),
            pl.BlockSpec((tm, A_WIDTH), row),
            pl.BlockSpec((tm, A_WIDTH), row),
            pl.BlockSpec((tm, B_WIDTH_K), row),
            pl.BlockSpec((tm, B_WIDTH_K), row),
            pl.BlockSpec((tm, B_WIDTH_V), row),
            pl.BlockSpec((tm, B_WIDTH_V), row),
            pl.BlockSpec((tm, B_WIDTH_K), row),
            pl.BlockSpec((A_HEADS, SGU_CHUNK, SGU_CHUNK), const3),
            pl.BlockSpec((SGU_CHUNK, A_WIDTH), const2),
            pl.BlockSpec((1, B_HEAD_V), const2),
            pl.BlockSpec((d, d), const2, pipeline_mode=single),
            pl.BlockSpec((1, d), const2),
        ],
        out_specs=pl.BlockSpec((tm, d), row),
        out_shape=jax.ShapeDtypeStruct((t, d), F32),
        scratch_shapes=[
            pltpu.VMEM((B_HEADS, B_HEAD_V, LANES), F32),
            pltpu.VMEM((tm, d), BF16),
        ],
        compiler_params=pltpu.CompilerParams(
            dimension_semantics=("arbitrary", "arbitrary"), vmem_limit_bytes=VMEM_LIMIT),
        name="mixer",
    )(x2, u, v, q, k, vb, sg, gk, w_s, bias_full, gn_g, w_out, n_post)


def _perm_base(r, n_groups):
    per = n_groups // SUBLANES
    return (r % per) * SUBLANES * SUBLANES + r // per


def _conv_ffn_kernel(x_ref, gpre_ref, wup_ref, cw_ref, cb_ref, wdn_ref, gpost_ref,
                     o_ref, carry_ref, hperm_ref, rperm_ref, y_ref):
    tm, d = x_ref.shape
    d_ff = wdn_ref.shape[0]
    ng = tm // SUBLANES
    nslab = d // LANES
    nblk = d_ff // FF_BLOCK

    @pl.when(pl.program_id(1) == 0)
    def _():
        carry_ref[...] = jnp.zeros_like(carry_ref)

    h32 = _rms_norm(x_ref[...], gpre_ref[...])
    for r in range(ng):
        base = _perm_base(r, ng)
        for sl in range(nslab):
            hperm_ref[sl, pl.ds(base, SUBLANES, stride=SUBLANES), :] = (
                h32[r * SUBLANES:(r + 1) * SUBLANES, sl * LANES:(sl + 1) * LANES])
    h = jnp.concatenate([hperm_ref[sl] for sl in range(nslab)], axis=-1).astype(BF16)

    sub = lax.broadcasted_iota(jnp.int32, (SUBLANES, FF_BLOCK), 0)

    def up(j):
        ca = slice(j * FF_BLOCK, (j + 1) * FF_BLOCK)
        cg = slice(d_ff + j * FF_BLOCK, d_ff + (j + 1) * FF_BLOCK)
        y_ref[j % 2, 0] = _dot(h, wup_ref[:, ca])
        y_ref[j % 2, 1] = _dot(h, wup_ref[:, cg])

    def conv(j, part):
        cols = slice(part * d_ff + j * FF_BLOCK, part * d_ff + (j + 1) * FF_BLOCK)
        yr = y_ref.at[j % 2, part]
        y = yr[...]
        prev = carry_ref[:, cols]
        last2 = yr[tm - 2 * SUBLANES:tm, :]
        carry_ref[0:1, cols] = last2[SUBLANES - 1:SUBLANES]
        carry_ref[1:2, cols] = last2[2 * SUBLANES - 1:2 * SUBLANES]
        f2 = jnp.where(sub == 0, prev[0:1], pltpu.roll(last2[:SUBLANES], 1, 0))
        f1 = jnp.where(sub == 0, prev[1:2], pltpu.roll(last2[SUBLANES:], 1, 0))
        y1 = jnp.concatenate([f1, y[:tm - SUBLANES]], axis=0)
        y2 = jnp.concatenate([f2, f1, y[:tm - 2 * SUBLANES]], axis=0)
        w = cw_ref[:, cols]
        return cb_ref[:, cols] + y2 * w[0:1] + y1 * w[1:2] + y * w[2:3]

    acc = jnp.zeros((tm, d), F32)
    up(0)
    for j in range(nblk):
        if j + 1 < nblk:
            up(j + 1)
        act = (_gelu_tanh(conv(j, 0)) * conv(j, 1)).astype(BF16)
        acc = acc + _dot(act, wdn_ref[j * FF_BLOCK:(j + 1) * FF_BLOCK, :])

    res = _rms_norm(acc, gpost_ref[...])
    for sl in range(nslab):
        rperm_ref[sl] = res[:, sl * LANES:(sl + 1) * LANES]
    for r in range(ng):
        base = _perm_base(r, ng)
        for sl in range(nslab):
            rows = slice(r * SUBLANES, (r + 1) * SUBLANES)
            cols = slice(sl * LANES, (sl + 1) * LANES)
            o_ref[rows, cols] = (x_ref[rows, cols]
                                 + rperm_ref[sl, pl.ds(base, SUBLANES, stride=SUBLANES), :])


def _conv_ffn(x2, g_pre, w_up, conv_w, conv_b, w_down, g_post, batch):
    t, d = x2.shape
    tm = FFN_TILE
    per_b = t // batch // tm
    d_ff = w_down.shape[0]
    row = lambda b, i: (b * per_b + i, 0)
    const = lambda b, i: (0, 0)
    single = pl.Buffered(1)
    return pl.pallas_call(
        _conv_ffn_kernel,
        grid=(batch, per_b),
        in_specs=[
            pl.BlockSpec((tm, d), row),
            pl.BlockSpec((1, d), const),
            pl.BlockSpec((d, 2 * d_ff), const, pipeline_mode=single),
            pl.BlockSpec((CONV_K, 2 * d_ff), const),
            pl.BlockSpec((1, 2 * d_ff), const),
            pl.BlockSpec((d_ff, d), const, pipeline_mode=single),
            pl.BlockSpec((1, d), const),
        ],
        out_specs=pl.BlockSpec((tm, d), row),
        out_shape=jax.ShapeDtypeStruct((t, d), F32),
        scratch_shapes=[
            pltpu.VMEM((CONV_K - 1, 2 * d_ff), F32),
            pltpu.VMEM((d // LANES, tm, LANES), F32),
            pltpu.VMEM((d // LANES, tm, LANES), F32),
            pltpu.VMEM((2, 2, tm, FF_BLOCK), F32),
        ],
        compiler_params=pltpu.CompilerParams(
            dimension_semantics=("arbitrary", "arbitrary"), vmem_limit_bytes=VMEM_LIMIT),
        name="conv_ffn",
    )(x2, g_pre, w_up, conv_w, conv_b, w_down, g_post)


def kernel(x, norm_mix_pre, w_in, sgu_ln_g, sgu_ln_b, sgu_w_s, sgu_b, gla_w_gk, gla_b_gk,
           gla_norm_g, w_out, norm_mix_post, norm_ffn_pre, w_up, conv_w, conv_b, w_down,
           norm_ffn_post):
    bsz, s, d = x.shape
    depth = w_in.shape[0]
    n_main = w_in.shape[2] - GATE_RANK
    x2 = x.reshape(bsz * s, d)
    for l in range(depth):
        w_main = w_in[l, :, :n_main].astype(BF16)
        w_lr = jnp.pad(w_in[l, :, n_main:], ((0, 0), (0, GATE_RANK_PAD - GATE_RANK)))
        w_gk = jnp.pad(gla_w_gk[l], ((0, GATE_RANK_PAD - GATE_RANK), (0, 0)))
        u, v, q, k, vb, sg, gk = _in_proj(
            x2, norm_mix_pre[l][None], w_main, w_lr, sgu_ln_g[l][None], sgu_ln_b[l][None],
            w_gk, gla_b_gk[l][None])
        bias_full = jnp.repeat(jnp.transpose(sgu_b[l]), A_HEAD_DIM, axis=1)
        x2 = _mixer(x2, u, v, q, k, vb, sg, gk, sgu_w_s[l], bias_full, gla_norm_g[l][None],
                    w_out[l].astype(BF16), norm_mix_post[l][None], bsz)
        x2 = _conv_ffn(x2, norm_ffn_pre[l][None], w_up[l].astype(BF16), conv_w[l],
                       conv_b[l][None], w_down[l].astype(BF16), norm_ffn_post[l][None], bsz)
    return x2.reshape(bsz, s, d)
```

```python
import functools
import math

import jax
import jax.numpy as jnp
from jax import lax
from jax.experimental import pallas as pl
from jax.experimental.pallas import tpu as pltpu

F32 = jnp.float32
BF16 = jnp.bfloat16

EPS = 1e-6
A_HEADS = 4
A_HEAD_DIM = 128
A_WIDTH = A_HEADS * A_HEAD_DIM
SGU_CHUNK = 128
B_HEADS = 4
B_HEAD_V = 128
B_HEAD_K = 64
B_WIDTH_V = B_HEADS * B_HEAD_V
B_WIDTH_K = B_HEADS * B_HEAD_K
GATE_RANK = 16
GATE_TAU = 16.0
CONV_K = 3

LANES = 128
SUBLANES = 8
GATE_RANK_PAD = LANES
GLA_BLOCK = 128
FF_BLOCK = 256
DOWN_GROUP = 4
VMEM_LIMIT = 56 * 1024 * 1024

IN_TILE = 1024
MIX_TILE = 1024
FFN_TILE = 512


def _rms_norm(x, g):
    ms = jnp.mean(x * x, axis=-1, keepdims=True)
    return x * lax.rsqrt(ms + EPS) * g


def _gelu_exact(x):
    return 0.5 * x * (1.0 + lax.erf(x * (1.0 / math.sqrt(2.0))))


def _gelu_tanh(x):
    c = math.sqrt(2.0 / math.pi)
    return 0.5 * x * (1.0 + jnp.tanh(c * (x + 0.044715 * (x * x * x))))


def _log_sigmoid(x):
    return jnp.minimum(x, 0.0) - jnp.log1p(jnp.exp(-jnp.abs(x)))


def _dot(a, b):
    return jnp.dot(a, b, preferred_element_type=F32)


def _dot_nt(a, b):
    return lax.dot_general(a, b, (((1,), (1,)), ((), ())), preferred_element_type=F32)


def _dot_tn(a, b):
    return lax.dot_general(a, b, (((0,), (0,)), ((), ())), preferred_element_type=F32)


def _in_proj_kernel(x_ref, g_ref, w_ref, wlr_ref, lng_ref, lnb_ref, wgk_ref, bgk_ref,
                    u_ref, v_ref, q_ref, k_ref, vb_ref, sg_ref, gk_ref, wg_ref, wbf_ref):
    @pl.when(pl.program_id(0) == 0)
    def _():
        wg_ref[...] = jnp.dot(wlr_ref[...], wgk_ref[...], preferred_element_type=F32,
                              precision=lax.Precision.HIGHEST).astype(BF16)
        wbf_ref[...] = w_ref[:, :wbf_ref.shape[1]].astype(BF16)

    h = _rms_norm(x_ref[...], g_ref[...]).astype(BF16)
    off_u = 0
    off_v = off_u + A_WIDTH
    off_q = off_v + A_WIDTH
    off_k = off_q + B_WIDTH_K
    off_vb = off_k + B_WIDTH_K
    off_g = off_vb + B_WIDTH_V

    u_ref[...] = _gelu_exact(_dot(h, wbf_ref[:, off_u:off_u + A_WIDTH])).astype(BF16)

    v = _gelu_exact(_dot(h, wbf_ref[:, off_v:off_v + A_WIDTH]))
    mu = jnp.mean(v, axis=-1, keepdims=True)
    vc = v - mu
    var = jnp.mean(vc * vc, axis=-1, keepdims=True)
    v_ref[...] = (vc * lax.rsqrt(var + EPS) * lng_ref[...] + lnb_ref[...]).astype(BF16)

    g_out = _dot(h, wbf_ref[:, off_g:off_g + B_WIDTH_V])
    sg_ref[...] = (g_out * jax.nn.sigmoid(g_out)).astype(BF16)

    pre = _dot(h, wg_ref[...]) + bgk_ref[...]
    gk_ref[...] = _log_sigmoid(pre) * (1.0 / GATE_TAU)

    vb_ref[...] = _dot(h, wbf_ref[:, off_vb:off_vb + B_WIDTH_V]).astype(BF16)
    q_ref[...] = _dot(h, wbf_ref[:, off_q:off_q + B_WIDTH_K]).astype(BF16)
    k_ref[...] = _dot(h, wbf_ref[:, off_k:off_k + B_WIDTH_K]).astype(BF16)


def _in_proj(x2, g, w_in, w_lr, ln_g, ln_b, w_gk, b_gk):
    t, d = x2.shape
    tm = IN_TILE
    n_in = w_in.shape[1]
    n_main = n_in - GATE_RANK
    row = lambda i: (i, 0)
    const = lambda i: (0, 0)
    single = pl.Buffered(1)
    return pl.pallas_call(
        _in_proj_kernel,
        grid=(t // tm,),
        in_specs=[
            pl.BlockSpec((tm, d), row),
            pl.BlockSpec((1, d), const),
            pl.BlockSpec((d, n_in), const, pipeline_mode=single),
            pl.BlockSpec((d, GATE_RANK_PAD), const, pipeline_mode=single),
            pl.BlockSpec((1, A_WIDTH), const),
            pl.BlockSpec((1, A_WIDTH), const),
            pl.BlockSpec((GATE_RANK_PAD, B_WIDTH_K), const),
            pl.BlockSpec((1, B_WIDTH_K), const),
        ],
        out_specs=[
            pl.BlockSpec((tm, A_WIDTH), row),
            pl.BlockSpec((tm, A_WIDTH), row),
            pl.BlockSpec((tm, B_WIDTH_K), row),
            pl.BlockSpec((tm, B_WIDTH_K), row),
            pl.BlockSpec((tm, B_WIDTH_V), row),
            pl.BlockSpec((tm, B_WIDTH_V), row),
            pl.BlockSpec((tm, B_WIDTH_K), row),
        ],
        out_shape=[
            jax.ShapeDtypeStruct((t, A_WIDTH), BF16),
            jax.ShapeDtypeStruct((t, A_WIDTH), BF16),
            jax.ShapeDtypeStruct((t, B_WIDTH_K), BF16),
            jax.ShapeDtypeStruct((t, B_WIDTH_K), BF16),
            jax.ShapeDtypeStruct((t, B_WIDTH_V), BF16),
            jax.ShapeDtypeStruct((t, B_WIDTH_V), BF16),
            jax.ShapeDtypeStruct((t, B_WIDTH_K), F32),
        ],
        scratch_shapes=[pltpu.VMEM((d, B_WIDTH_K), BF16), pltpu.VMEM((d, n_main), BF16)],
        compiler_params=pltpu.CompilerParams(
            dimension_semantics=("arbitrary",), vmem_limit_bytes=VMEM_LIMIT),
        name="in_proj",
    )(x2, g, w_in, w_lr, ln_g, ln_b, w_gk, b_gk)


def _mixer_kernel(x_ref, u_ref, v_ref, q_ref, k_ref, vb_ref, sg_ref, gk_ref,
                  ws_ref, bias_ref, gng_ref, wout_ref, npost_ref,
                  o_ref, state_ref, mix_ref, wout_bf_ref):
    tm = x_ref.shape[0]

    @pl.when((pl.program_id(0) == 0) & (pl.program_id(1) == 0))
    def _():
        wout_bf_ref[...] = wout_ref[...].astype(BF16)

    @pl.when(pl.program_id(1) == 0)
    def _():
        state_ref[...] = jnp.zeros_like(state_ref)

    r = lax.broadcasted_iota(jnp.int32, (SGU_CHUNK, SGU_CHUNK), 0)
    c = lax.broadcasted_iota(jnp.int32, (SGU_CHUNK, SGU_CHUNK), 1)
    causal = r >= c
    for h in range(A_HEADS):
        cols = slice(h * A_HEAD_DIM, (h + 1) * A_HEAD_DIM)
        w_m = jnp.where(causal, ws_ref[h], 0.0).astype(BF16)
        for ch in range(tm // SGU_CHUNK):
            rows = slice(ch * SGU_CHUNK, (ch + 1) * SGU_CHUNK)
            z = _dot(w_m, v_ref[rows, cols]) + bias_ref[:, cols]
            mix_ref[rows, cols] = (u_ref[rows, cols].astype(F32) * z).astype(BF16)

    cb = GLA_BLOCK
    ri = lax.broadcasted_iota(jnp.int32, (cb, cb), 0)
    ci = lax.broadcasted_iota(jnp.int32, (cb, cb), 1)
    tri_incl = ri >= ci
    tri_bf = jnp.where(tri_incl, 1.0, 0.0).astype(BF16)
    lane = lax.broadcasted_iota(jnp.int32, (1, LANES), 1)
    scale = B_HEAD_K ** -0.5
    mid = cb // 2 - 1
    states = [state_ref[h] for h in range(B_HEADS)]
    for blk in range(tm // cb):
        rows = slice(blk * cb, (blk + 1) * cb)
        g = gk_ref[rows, :]
        g_hi = g.astype(BF16)
        g_lo = (g - g_hi.astype(F32)).astype(BF16)
        b = _dot(tri_bf, g_hi) + _dot(tri_bf, g_lo)
        b_mid = b[mid:mid + 1, :]
        b_last = b[cb - 1:cb, :]
        e_pos = jnp.exp(b - b_mid)
        e_neg = jnp.exp(b_mid - b)
        qt = q_ref[rows, :].astype(F32) * scale * e_pos
        kt = k_ref[rows, :].astype(F32) * e_neg
        q_in = (qt * jnp.exp(b_mid)).astype(BF16)
        k_dec = kt * jnp.exp(b_last - b_mid)
        s_decay = jnp.exp(b_last)
        qt = qt.astype(BF16)
        for h in range(B_HEADS):
            tile = slice((h // 2) * LANES, (h // 2 + 1) * LANES)
            in_head = (lane >= (h % 2) * B_HEAD_K) & (lane < (h % 2 + 1) * B_HEAD_K)
            k_m = jnp.where(in_head, kt[:, tile], 0.0).astype(BF16)
            kd_m = jnp.where(in_head, k_dec[:, tile], 0.0).astype(BF16)
            vcols = slice(h * B_HEAD_V, (h + 1) * B_HEAD_V)
            v_h = vb_ref[rows, vcols]
            att = _dot_nt(qt[:, tile], k_m)
            att = jnp.where(tri_incl, att, 0.0).astype(BF16)
            o = _dot(att, v_h) + _dot_nt(q_in[:, tile], states[h].astype(BF16))
            states[h] = states[h] * s_decay[:, tile] + _dot_tn(v_h, kd_m)
            o = _rms_norm(o, gng_ref[...]) * sg_ref[rows, vcols].astype(F32)
            mix_ref[rows, A_WIDTH + h * B_HEAD_V:A_WIDTH + (h + 1) * B_HEAD_V] = o.astype(BF16)
    for h in range(B_HEADS):
        state_ref[h] = states[h]

    mix = _dot(mix_ref[...], wout_bf_ref[...])
    o_ref[...] = x_ref[...] + _rms_norm(mix, npost_ref[...])


def _mixer(x2, u, v, q, k, vb, sg, gk, w_s, bias_full, gn_g, w_out, n_post, batch):
    t, d = x2.shape
    tm = MIX_TILE
    per_b = t // batch // tm
    row = lambda b, i: (b * per_b + i, 0)
    const2 = lambda b, i: (0, 0)
    const3 = lambda b, i: (0, 0, 0)
    single = pl.Buffered(1)
    return pl.pallas_call(
        _mixer_kernel,
        grid=(batch, per_b),
        in_specs=[
            pl.BlockSpec((tm, d), row),
            pl.BlockSpec((tm, A_WIDTH), row),
            pl.BlockSpec((tm, A_WIDTH), row),
            pl.BlockSpec((tm, B_WIDTH_K), row),
            pl.BlockSpec((tm, B_WIDTH_K), row),
            pl.BlockSpec((tm, B_WIDTH_V), row),
            pl.BlockSpec((tm, B_WIDTH_V), row),
            pl.BlockSpec((tm, B_WIDTH_K), row),
            pl.BlockSpec((A_HEADS, SGU_CHUNK, SGU_CHUNK), const3),
            pl.BlockSpec((SGU_CHUNK, A_WIDTH), const2),
            pl.BlockSpec((1, B_HEAD_V), const2),
            pl.BlockSpec((d, d), const2, pipeline_mode=single),
            pl.BlockSpec((1, d), const2),
        ],
        out_specs=pl.BlockSpec((tm, d), row),
        out_shape=jax.ShapeDtypeStruct((t, d), F32),
        scratch_shapes=[
            pltpu.VMEM((B_HEADS, B_HEAD_V, LANES), F32),
            pltpu.VMEM((tm, d), BF16),
            pltpu.VMEM((d, d), BF16),
        ],
        compiler_params=pltpu.CompilerParams(
            dimension_semantics=("arbitrary", "arbitrary"), vmem_limit_bytes=VMEM_LIMIT),
        name="mixer",
    )(x2, u, v, q, k, vb, sg, gk, w_s, bias_full, gn_g, w_out, n_post)


def _perm_base(r, n_groups):
    per = n_groups // SUBLANES
    return (r % per) * SUBLANES * SUBLANES + r // per


def _conv_ffn_kernel(x_ref, gpre_ref, wup_ref, cw_ref, cb_ref, wdn_ref, gpost_ref,
                     o_ref, carry_ref, hperm_ref, rperm_ref, y_ref, wdn_bf_ref):
    tm, d = x_ref.shape
    d_ff = wdn_ref.shape[0]
    ng = tm // SUBLANES
    nslab = d // LANES
    nblk = d_ff // FF_BLOCK

    @pl.when((pl.program_id(0) == 0) & (pl.program_id(1) == 0))
    def _():
        wdn_bf_ref[...] = wdn_ref[...].astype(BF16)

    @pl.when(pl.program_id(1) == 0)
    def _():
        carry_ref[...] = jnp.zeros_like(carry_ref)

    h32 = _rms_norm(x_ref[...], gpre_ref[...])
    for r in range(ng):
        base = _perm_base(r, ng)
        for sl in range(nslab):
            hperm_ref[sl, pl.ds(base, SUBLANES, stride=SUBLANES), :] = (
                h32[r * SUBLANES:(r + 1) * SUBLANES, sl * LANES:(sl + 1) * LANES])
    h = jnp.concatenate([hperm_ref[sl] for sl in range(nslab)], axis=-1).astype(BF16)

    sub = lax.broadcasted_iota(jnp.int32, (SUBLANES, FF_BLOCK), 0)

    def up(j):
        ca = slice(j * FF_BLOCK, (j + 1) * FF_BLOCK)
        cg = slice(d_ff + j * FF_BLOCK, d_ff + (j + 1) * FF_BLOCK)
        y_ref[j % 2, 0] = _dot(h, wup_ref[:, ca])
        y_ref[j % 2, 1] = _dot(h, wup_ref[:, cg])

    def conv(j, part):
        cols = slice(part * d_ff + j * FF_BLOCK, part * d_ff + (j + 1) * FF_BLOCK)
        yr = y_ref.at[j % 2, part]
        y = yr[...]
        prev = carry_ref[:, cols]
        last2 = yr[tm - 2 * SUBLANES:tm, :]
        carry_ref[0:1, cols] = last2[SUBLANES - 1:SUBLANES]
        carry_ref[1:2, cols] = last2[2 * SUBLANES - 1:2 * SUBLANES]
        f2 = jnp.where(sub == 0, prev[0:1], pltpu.roll(last2[:SUBLANES], 1, 0))
        f1 = jnp.where(sub == 0, prev[1:2], pltpu.roll(last2[SUBLANES:], 1, 0))
        y1 = jnp.concatenate([f1, y[:tm - SUBLANES]], axis=0)
        y2 = jnp.concatenate([f2, f1, y[:tm - 2 * SUBLANES]], axis=0)
        w = cw_ref[:, cols]
        return cb_ref[:, cols] + y2 * w[0:1] + y1 * w[1:2] + y * w[2:3]

    acc = jnp.zeros((tm, d), F32)
    up(0)
    acts = []
    for j in range(nblk):
        if j + 1 < nblk:
            up(j + 1)
        acts.append((_gelu_tanh(conv(j, 0)) * conv(j, 1)).astype(BF16))
        if len(acts) == DOWN_GROUP or j == nblk - 1:
            lo = (j + 1 - len(acts)) * FF_BLOCK
            act = acts[0] if len(acts) == 1 else jnp.concatenate(acts, axis=-1)
            acc = acc + _dot(act, wdn_bf_ref[lo:(j + 1) * FF_BLOCK, :])
            acts = []

    res = _rms_norm(acc, gpost_ref[...])
    for sl in range(nslab):
        rperm_ref[sl] = res[:, sl * LANES:(sl + 1) * LANES]
    for r in range(ng):
        base = _perm_base(r, ng)
        for sl in range(nslab):
            rows = slice(r * SUBLANES, (r + 1) * SUBLANES)
            cols = slice(sl * LANES, (sl + 1) * LANES)
            o_ref[rows, cols] = (x_ref[rows, cols]
                                 + rperm_ref[sl, pl.ds(base, SUBLANES, stride=SUBLANES), :])


def _conv_ffn(x2, g_pre, w_up, conv_w, conv_b, w_down, g_post, batch):
    t, d = x2.shape
    tm = FFN_TILE
    per_b = t // batch // tm
    d_ff = w_down.shape[0]
    row = lambda b, i: (b * per_b + i, 0)
    const = lambda b, i: (0, 0)
    single = pl.Buffered(1)
    return pl.pallas_call(
        _conv_ffn_kernel,
        grid=(batch, per_b),
        in_specs=[
            pl.BlockSpec((tm, d), row),
            pl.BlockSpec((1, d), const),
            pl.BlockSpec((d, 2 * d_ff), const, pipeline_mode=single),
            pl.BlockSpec((CONV_K, 2 * d_ff), const),
            pl.BlockSpec((1, 2 * d_ff), const),
            pl.BlockSpec((d_ff, d), const, pipeline_mode=single),
            pl.BlockSpec((1, d), const),
        ],
        out_specs=pl.BlockSpec((tm, d), row),
        out_shape=jax.ShapeDtypeStruct((t, d), F32),
        scratch_shapes=[
            pltpu.VMEM((CONV_K - 1, 2 * d_ff), F32),
            pltpu.VMEM((d // LANES, tm, LANES), F32),
            pltpu.VMEM((d // LANES, tm, LANES), F32),
            pltpu.VMEM((2, 2, tm, FF_BLOCK), F32),
            pltpu.VMEM((d_ff, d), BF16),
        ],
        compiler_params=pltpu.CompilerParams(
            dimension_semantics=("arbitrary", "arbitrary"), vmem_limit_bytes=VMEM_LIMIT),
        name="conv_ffn",
    )(x2, g_pre, w_up, conv_w, conv_b, w_down, g_post)


def kernel(x, norm_mix_pre, w_in, sgu_ln_g, sgu_ln_b, sgu_w_s, sgu_b, gla_w_gk, gla_b_gk,
           gla_norm_g, w_out, norm_mix_post, norm_ffn_pre, w_up, conv_w, conv_b, w_down,
           norm_ffn_post):
    bsz, s, d = x.shape
    depth = w_in.shape[0]
    n_main = w_in.shape[2] - GATE_RANK
    x2 = x.reshape(bsz * s, d)
    for l in range(depth):
        w_lr = jnp.pad(w_in[l, :, n_main:], ((0, 0), (0, GATE_RANK_PAD - GATE_RANK)))
        w_gk = jnp.pad(gla_w_gk[l], ((0, GATE_RANK_PAD - GATE_RANK), (0, 0)))
        u, v, q, k, vb, sg, gk = _in_proj(
            x2, norm_mix_pre[l][None], w_in[l], w_lr, sgu_ln_g[l][None], sgu_ln_b[l][None],
            w_gk, gla_b_gk[l][None])
        bias_full = jnp.repeat(jnp.transpose(sgu_b[l]), A_HEAD_DIM, axis=1)
        x2 = _mixer(x2, u, v, q, k, vb, sg, gk, sgu_w_s[l], bias_full, gla_norm_g[l][None],
                    w_out[l], norm_mix_post[l][None], bsz)
        x2 = _conv_ffn(x2, norm_ffn_pre[l][None], w_up[l].astype(BF16), conv_w[l],
                       conv_b[l][None], w_down[l], norm_ffn_post[l][None], bsz)
    return x2.reshape(bsz, s, d)
```

```python
import functools
import math

import jax
import jax.numpy as jnp
from jax import lax
from jax.experimental import pallas as pl
from jax.experimental.pallas import tpu as pltpu

F32 = jnp.float32
BF16 = jnp.bfloat16

EPS = 1e-6
A_HEADS = 4
A_HEAD_DIM = 128
A_WIDTH = A_HEADS * A_HEAD_DIM
SGU_CHUNK = 128
B_HEADS = 4
B_HEAD_V = 128
B_HEAD_K = 64
B_WIDTH_V = B_HEADS * B_HEAD_V
B_WIDTH_K = B_HEADS * B_HEAD_K
GATE_RANK = 16
GATE_TAU = 16.0
CONV_K = 3

LANES = 128
SUBLANES = 8
GATE_RANK_PAD = LANES
GLA_BLOCK = 256
FF_BLOCK = 256
DOWN_GROUP = 4
GLA_MIDPOINT_MAX_DECAY = 60.0
VMEM_LIMIT = 56 * 1024 * 1024

IN_TILE = 1024
MIX_TILE = 1024
FFN_TILE = 512


def _rms_norm(x, g):
    ms = jnp.mean(x * x, axis=-1, keepdims=True)
    return x * lax.rsqrt(ms + EPS) * g


def _gelu_exact(x):
    return 0.5 * x * (1.0 + lax.erf(x * (1.0 / math.sqrt(2.0))))


def _gelu_tanh(x):
    c = math.sqrt(2.0 / math.pi)
    return 0.5 * x * (1.0 + jnp.tanh(c * (x + 0.044715 * (x * x * x))))


def _log_sigmoid(x):
    return jnp.minimum(x, 0.0) - jnp.log1p(jnp.exp(-jnp.abs(x)))


def _dot(a, b):
    return jnp.dot(a, b, preferred_element_type=F32)


def _dot_nt(a, b):
    return lax.dot_general(a, b, (((1,), (1,)), ((), ())), preferred_element_type=F32)


def _dot_tn(a, b):
    return lax.dot_general(a, b, (((0,), (0,)), ((), ())), preferred_element_type=F32)


def _in_proj_kernel(x_ref, g_ref, w_ref, wlr_ref, lng_ref, lnb_ref, wgk_ref, bgk_ref,
                    u_ref, v_ref, q_ref, k_ref, vb_ref, sg_ref, gk_ref, wg_ref, wbf_ref):
    @pl.when(pl.program_id(0) == 0)
    def _():
        wg_ref[...] = jnp.dot(wlr_ref[...], wgk_ref[...], preferred_element_type=F32,
                              precision=lax.Precision.HIGHEST).astype(BF16)
        wbf_ref[...] = w_ref[:, :wbf_ref.shape[1]].astype(BF16)

    h = _rms_norm(x_ref[...], g_ref[...]).astype(BF16)
    off_u = 0
    off_v = off_u + A_WIDTH
    off_q = off_v + A_WIDTH
    off_k = off_q + B_WIDTH_K
    off_vb = off_k + B_WIDTH_K
    off_g = off_vb + B_WIDTH_V

    u_ref[...] = _gelu_exact(_dot(h, wbf_ref[:, off_u:off_u + A_WIDTH])).astype(BF16)

    v = _gelu_exact(_dot(h, wbf_ref[:, off_v:off_v + A_WIDTH]))
    mu = jnp.mean(v, axis=-1, keepdims=True)
    vc = v - mu
    var = jnp.mean(vc * vc, axis=-1, keepdims=True)
    v_ref[...] = (vc * lax.rsqrt(var + EPS) * lng_ref[...] + lnb_ref[...]).astype(BF16)

    g_out = _dot(h, wbf_ref[:, off_g:off_g + B_WIDTH_V])
    sg_ref[...] = (g_out * jax.nn.sigmoid(g_out)).astype(BF16)

    pre = _dot(h, wg_ref[...]) + bgk_ref[...]
    gk_ref[...] = _log_sigmoid(pre) * (1.0 / GATE_TAU)

    vb_ref[...] = _dot(h, wbf_ref[:, off_vb:off_vb + B_WIDTH_V]).astype(BF16)
    q_ref[...] = _dot(h, wbf_ref[:, off_q:off_q + B_WIDTH_K]).astype(BF16)
    k_ref[...] = _dot(h, wbf_ref[:, off_k:off_k + B_WIDTH_K]).astype(BF16)


def _in_proj(x2, g, w_in, w_lr, ln_g, ln_b, w_gk, b_gk):
    t, d = x2.shape
    tm = IN_TILE
    n_in = w_in.shape[1]
    n_main = n_in - GATE_RANK
    row = lambda i: (i, 0)
    const = lambda i: (0, 0)
    single = pl.Buffered(1)
    return pl.pallas_call(
        _in_proj_kernel,
        grid=(t // tm,),
        in_specs=[
            pl.BlockSpec((tm, d), row),
            pl.BlockSpec((1, d), const),
            pl.BlockSpec((d, n_in), const, pipeline_mode=single),
            pl.BlockSpec((d, GATE_RANK_PAD), const, pipeline_mode=single),
            pl.BlockSpec((1, A_WIDTH), const),
            pl.BlockSpec((1, A_WIDTH), const),
            pl.BlockSpec((GATE_RANK_PAD, B_WIDTH_K), const),
            pl.BlockSpec((1, B_WIDTH_K), const),
        ],
        out_specs=[
            pl.BlockSpec((tm, A_WIDTH), row),
            pl.BlockSpec((tm, A_WIDTH), row),
            pl.BlockSpec((tm, B_WIDTH_K), row),
            pl.BlockSpec((tm, B_WIDTH_K), row),
            pl.BlockSpec((tm, B_WIDTH_V), row),
            pl.BlockSpec((tm, B_WIDTH_V), row),
            pl.BlockSpec((tm, B_WIDTH_K), row),
        ],
        out_shape=[
            jax.ShapeDtypeStruct((t, A_WIDTH), BF16),
            jax.ShapeDtypeStruct((t, A_WIDTH), BF16),
            jax.ShapeDtypeStruct((t, B_WIDTH_K), BF16),
            jax.ShapeDtypeStruct((t, B_WIDTH_K), BF16),
            jax.ShapeDtypeStruct((t, B_WIDTH_V), BF16),
            jax.ShapeDtypeStruct((t, B_WIDTH_V), BF16),
            jax.ShapeDtypeStruct((t, B_WIDTH_K), F32),
        ],
        scratch_shapes=[pltpu.VMEM((d, B_WIDTH_K), BF16), pltpu.VMEM((d, n_main), BF16)],
        compiler_params=pltpu.CompilerParams(
            dimension_semantics=("arbitrary",), vmem_limit_bytes=VMEM_LIMIT),
        name="in_proj",
    )(x2, g, w_in, w_lr, ln_g, ln_b, w_gk, b_gk)


def _gla_block(rows, states, safe, q_ref, k_ref, vb_ref, sg_ref, gk_ref, gng_ref, mix_ref):
    cb = GLA_BLOCK
    ri = lax.broadcasted_iota(jnp.int32, (cb, cb), 0)
    ci = lax.broadcasted_iota(jnp.int32, (cb, cb), 1)
    tri_incl = ri >= ci
    lane = lax.broadcasted_iota(jnp.int32, (1, LANES), 1)
    scale = B_HEAD_K ** -0.5

    def block_sum(mask, g_hi, g_lo):
        m = jnp.where(mask, 1.0, 0.0).astype(BF16)
        return _dot(m, g_hi) + _dot(m, g_lo)

    g = gk_ref[rows, :]
    g_hi = g.astype(BF16)
    g_lo = (g - g_hi.astype(F32)).astype(BF16)
    b = block_sum(tri_incl, g_hi, g_lo)
    b_last = b[cb - 1:cb, :]
    q = q_ref[rows, :].astype(F32) * scale
    k = k_ref[rows, :].astype(F32)
    s_decay = jnp.exp(b_last)

    def head_lanes(h):
        tile = slice((h // 2) * LANES, (h // 2 + 1) * LANES)
        in_head = (lane >= (h % 2) * B_HEAD_K) & (lane < (h % 2 + 1) * B_HEAD_K)
        return tile, in_head

    if not safe:
        mid = cb // 2 - 1
        b_mid = b[mid:mid + 1, :]
        qt = q * jnp.exp(b - b_mid)
        kt = k * jnp.exp(b_mid - b)
        q_in = (qt * jnp.exp(b_mid)).astype(BF16)
        k_dec = kt * jnp.exp(b_last - b_mid)
        qt = qt.astype(BF16)
        atts = []
        for h in range(B_HEADS):
            tile, in_head = head_lanes(h)
            k_m = jnp.where(in_head, kt[:, tile], 0.0).astype(BF16)
            atts.append(jnp.where(tri_incl, _dot_nt(qt[:, tile], k_m), 0.0))
    else:
        q_in = (q * jnp.exp(b)).astype(BF16)
        k_dec = k * jnp.exp(b_last - b)
        atts = []
        for h in range(B_HEADS):
            tile, in_head = head_lanes(h)
            k_m = jnp.where(in_head, k[:, tile], 0.0).astype(BF16)
            atts.append(jnp.where(ri == ci, _dot_nt(q[:, tile].astype(BF16), k_m), 0.0))
        s = cb // 2
        while s >= 1:
            same = (ri // s) == (ci // s)
            c_s = block_sum(same & tri_incl, g_hi, g_lo)
            t_s = block_sum(same, g_hi, g_lo)
            q_s = (q * jnp.exp(c_s)).astype(BF16)
            k_s = k * jnp.exp(t_s - c_s)
            pair = (((ri // s) % 2) == 1) & ((ci // s) == (ri // s) - 1)
            for h in range(B_HEADS):
                tile, in_head = head_lanes(h)
                k_m = jnp.where(in_head, k_s[:, tile], 0.0).astype(BF16)
                atts[h] = atts[h] + jnp.where(pair, _dot_nt(q_s[:, tile], k_m), 0.0)
            s //= 2

    new_states = []
    for h in range(B_HEADS):
        tile, in_head = head_lanes(h)
        kd_m = jnp.where(in_head, k_dec[:, tile], 0.0).astype(BF16)
        vcols = slice(h * B_HEAD_V, (h + 1) * B_HEAD_V)
        v_h = vb_ref[rows, vcols]
        o = _dot(atts[h].astype(BF16), v_h) + _dot_nt(q_in[:, tile], states[h].astype(BF16))
        new_states.append(states[h] * s_decay[:, tile] + _dot_tn(v_h, kd_m))
        o = _rms_norm(o, gng_ref[...]) * sg_ref[rows, vcols].astype(F32)
        mix_ref[rows, A_WIDTH + h * B_HEAD_V:A_WIDTH + (h + 1) * B_HEAD_V] = o.astype(BF16)
    return new_states


def _mixer_kernel(x_ref, u_ref, v_ref, q_ref, k_ref, vb_ref, sg_ref, gk_ref,
                  ws_ref, bias_ref, gng_ref, wout_ref, npost_ref,
                  o_ref, state_ref, mix_ref, wout_bf_ref):
    tm = x_ref.shape[0]
    cb = GLA_BLOCK
    n_blk = tm // cb

    @pl.when((pl.program_id(0) == 0) & (pl.program_id(1) == 0))
    def _():
        wout_bf_ref[...] = wout_ref[...].astype(BF16)

    @pl.when(pl.program_id(1) == 0)
    def _():
        state_ref[...] = jnp.zeros_like(state_ref)

    r = lax.broadcasted_iota(jnp.int32, (SGU_CHUNK, SGU_CHUNK), 0)
    c = lax.broadcasted_iota(jnp.int32, (SGU_CHUNK, SGU_CHUNK), 1)
    n_ch = tm // SGU_CHUNK
    for h in range(A_HEADS):
        cols = slice(h * A_HEAD_DIM, (h + 1) * A_HEAD_DIM)
        w_m = jnp.where(r >= c, ws_ref[h], 0.0).astype(BF16)
        v_all = jnp.concatenate(
            [v_ref[ch * SGU_CHUNK:(ch + 1) * SGU_CHUNK, cols] for ch in range(n_ch)], axis=1)
        z_all = _dot(w_m, v_all)
        for ch in range(n_ch):
            rows = slice(ch * SGU_CHUNK, (ch + 1) * SGU_CHUNK)
            z = z_all[:, ch * A_HEAD_DIM:(ch + 1) * A_HEAD_DIM] + bias_ref[:, cols]
            mix_ref[rows, cols] = (u_ref[rows, cols].astype(F32) * z).astype(BF16)

    worst = jnp.zeros((1, B_WIDTH_K), F32)
    for blk in range(n_blk):
        worst = jnp.maximum(worst, -jnp.sum(gk_ref[blk * cb:(blk + 1) * cb, :], axis=0, keepdims=True))
    bounded = jnp.max(worst) < GLA_MIDPOINT_MAX_DECAY

    gla_refs = (q_ref, k_ref, vb_ref, sg_ref, gk_ref, gng_ref, mix_ref)

    def merge():
        mix = _dot(mix_ref[...], wout_bf_ref[...])
        o_ref[...] = x_ref[...] + _rms_norm(mix, npost_ref[...])

    def fast():
        states = [state_ref[h] for h in range(B_HEADS)]
        for blk in range(n_blk):
            states = _gla_block(slice(blk * cb, (blk + 1) * cb), states, False, *gla_refs)
        for h in range(B_HEADS):
            state_ref[h] = states[h]
        merge()

    def slow():
        def body(blk, states):
            rows = pl.ds(pl.multiple_of(blk * cb, cb), cb)
            return tuple(_gla_block(rows, list(states), True, *gla_refs))
        states = lax.fori_loop(0, n_blk, body, tuple(state_ref[h] for h in range(B_HEADS)))
        for h in range(B_HEADS):
            state_ref[h] = states[h]
        merge()

    lax.cond(bounded, fast, slow)


def _mixer(x2, u, v, q, k, vb, sg, gk, w_s, bias_full, gn_g, w_out, n_post, batch):
    t, d = x2.shape
    tm = MIX_TILE
    per_b = t // batch // tm
    row = lambda b, i: (b * per_b + i, 0)
    const2 = lambda b, i: (0, 0)
    const3 = lambda b, i: (0, 0, 0)
    single = pl.Buffered(1)
    return pl.pallas_call(
        _mixer_kernel,
        grid=(batch, per_b),
        in_specs=[
            pl.BlockSpec((tm, d), row),
            pl.BlockSpec((tm, A_WIDTH), row),
            pl.BlockSpec((tm, A_WIDTH), row),
            pl.BlockSpec((tm, B_WIDTH_K), row),
            pl.BlockSpec((tm, B_WIDTH_K), row),
            pl.BlockSpec((tm, B_WIDTH_V), row),
            pl.BlockSpec((tm, B_WIDTH_V), row),
            pl.BlockSpec((tm, B_WIDTH_K), row),
            pl.BlockSpec((A_HEADS, SGU_CHUNK, SGU_CHUNK), const3),
            pl.BlockSpec((SGU_CHUNK, A_WIDTH), const2),
            pl.BlockSpec((1, B_HEAD_V), const2),
            pl.BlockSpec((d, d), const2, pipeline_mode=single),
            pl.BlockSpec((1, d), const2),
        ],
        out_specs=pl.BlockSpec((tm, d), row),
        out_shape=jax.ShapeDtypeStruct((t, d), F32),
        scratch_shapes=[
            pltpu.VMEM((B_HEADS, B_HEAD_V, LANES), F32),
            pltpu.VMEM((tm, d), BF16),
            pltpu.VMEM((d, d), BF16),
        ],
        compiler_params=pltpu.CompilerParams(
            dimension_semantics=("arbitrary", "arbitrary"), vmem_limit_bytes=VMEM_LIMIT),
        name="mixer",
    )(x2, u, v, q, k, vb, sg, gk, w_s, bias_full, gn_g, w_out, n_post)


def _perm_base(r, n_groups):
    per = n_groups // SUBLANES
    return (r % per) * SUBLANES * SUBLANES + r // per


def _conv_ffn_kernel(x_ref, gpre_ref, wup_ref, cw_ref, cb_ref, wdn_ref, gpost_ref,
                     o_ref, carry_ref, hperm_ref, rperm_ref, y_ref, wdn_bf_ref):
    tm, d = x_ref.shape
    d_ff = wdn_ref.shape[0]
    ng = tm // SUBLANES
    nslab = d // LANES
    nblk = d_ff // FF_BLOCK

    @pl.when((pl.program_id(0) == 0) & (pl.program_id(1) == 0))
    def _():
        wdn_bf_ref[...] = wdn_ref[...].astype(BF16)

    @pl.when(pl.program_id(1) == 0)
    def _():
        carry_ref[...] = jnp.zeros_like(carry_ref)

    h32 = _rms_norm(x_ref[...], gpre_ref[...])
    for r in range(ng):
        base = _perm_base(r, ng)
        for sl in range(nslab):
            hperm_ref[sl, pl.ds(base, SUBLANES, stride=SUBLANES), :] = (
                h32[r * SUBLANES:(r + 1) * SUBLANES, sl * LANES:(sl + 1) * LANES])
    h = jnp.concatenate([hperm_ref[sl] for sl in range(nslab)], axis=-1).astype(BF16)

    sub = lax.broadcasted_iota(jnp.int32, (SUBLANES, FF_BLOCK), 0)

    def up(j):
        ca = slice(j * FF_BLOCK, (j + 1) * FF_BLOCK)
        cg = slice(d_ff + j * FF_BLOCK, d_ff + (j + 1) * FF_BLOCK)
        y_ref[j % 2, 0] = _dot(h, wup_ref[:, ca])
        y_ref[j % 2, 1] = _dot(h, wup_ref[:, cg])

    def conv(j, part):
        cols = slice(part * d_ff + j * FF_BLOCK, part * d_ff + (j + 1) * FF_BLOCK)
        yr = y_ref.at[j % 2, part]
        y = yr[...]
        prev = carry_ref[:, cols]
        last2 = yr[tm - 2 * SUBLANES:tm, :]
        carry_ref[0:1, cols] = last2[SUBLANES - 1:SUBLANES]
        carry_ref[1:2, cols] = last2[2 * SUBLANES - 1:2 * SUBLANES]
        f2 = jnp.where(sub == 0, prev[0:1], pltpu.roll(last2[:SUBLANES], 1, 0))
        f1 = jnp.where(sub == 0, prev[1:2], pltpu.roll(last2[SUBLANES:], 1, 0))
        y1 = jnp.concatenate([f1, y[:tm - SUBLANES]], axis=0)
        y2 = jnp.concatenate([f2, f1, y[:tm - 2 * SUBLANES]], axis=0)
        w = cw_ref[:, cols]
        return cb_ref[:, cols] + y2 * w[0:1] + y1 * w[1:2] + y * w[2:3]

    acc = jnp.zeros((tm, d), F32)
    up(0)
    acts = []
    for j in range(nblk):
        if j + 1 < nblk:
            up(j + 1)
        acts.append((_gelu_tanh(conv(j, 0)) * conv(j, 1)).astype(BF16))
        if len(acts) == DOWN_GROUP or j == nblk - 1:
            lo = (j + 1 - len(acts)) * FF_BLOCK
            act = acts[0] if len(acts) == 1 else jnp.concatenate(acts, axis=-1)
            acc = acc + _dot(act, wdn_bf_ref[lo:(j + 1) * FF_BLOCK, :])
            acts = []

    res = _rms_norm(acc, gpost_ref[...])
    for sl in range(nslab):
        rperm_ref[sl] = res[:, sl * LANES:(sl + 1) * LANES]
    for r in range(ng):
        base = _perm_base(r, ng)
        for sl in range(nslab):
            rows = slice(r * SUBLANES, (r + 1) * SUBLANES)
            cols = slice(sl * LANES, (sl + 1) * LANES)
            o_ref[rows, cols] = (x_ref[rows, cols]
                                 + rperm_ref[sl, pl.ds(base, SUBLANES, stride=SUBLANES), :])


def _conv_ffn(x2, g_pre, w_up, conv_w, conv_b, w_down, g_post, batch):
    t, d = x2.shape
    tm = FFN_TILE
    per_b = t // batch // tm
    d_ff = w_down.shape[0]
    row = lambda b, i: (b * per_b + i, 0)
    const = lambda b, i: (0, 0)
    single = pl.Buffered(1)
    return pl.pallas_call(
        _conv_ffn_kernel,
        grid=(batch, per_b),
        in_specs=[
            pl.BlockSpec((tm, d), row),
            pl.BlockSpec((1, d), const),
            pl.BlockSpec((d, 2 * d_ff), const, pipeline_mode=single),
            pl.BlockSpec((CONV_K, 2 * d_ff), const),
            pl.BlockSpec((1, 2 * d_ff), const),
            pl.BlockSpec((d_ff, d), const, pipeline_mode=single),
            pl.BlockSpec((1, d), const),
        ],
        out_specs=pl.BlockSpec((tm, d), row),
        out_shape=jax.ShapeDtypeStruct((t, d), F32),
        scratch_shapes=[
            pltpu.VMEM((CONV_K - 1, 2 * d_ff), F32),
            pltpu.VMEM((d // LANES, tm, LANES), F32),
            pltpu.VMEM((d // LANES, tm, LANES), F32),
            pltpu.VMEM((2, 2, tm, FF_BLOCK), F32),
            pltpu.VMEM((d_ff, d), BF16),
        ],
        compiler_params=pltpu.CompilerParams(
            dimension_semantics=("arbitrary", "arbitrary"), vmem_limit_bytes=VMEM_LIMIT),
        name="conv_ffn",
    )(x2, g_pre, w_up, conv_w, conv_b, w_down, g_post)


def kernel(x, norm_mix_pre, w_in, sgu_ln_g, sgu_ln_b, sgu_w_s, sgu_b, gla_w_gk, gla_b_gk,
           gla_norm_g, w_out, norm_mix_post, norm_ffn_pre, w_up, conv_w, conv_b, w_down,
           norm_ffn_post):
    bsz, s, d = x.shape
    depth = w_in.shape[0]
    n_main = w_in.shape[2] - GATE_RANK
    x2 = x.reshape(bsz * s, d)
    for l in range(depth):
        w_lr = jnp.pad(w_in[l, :, n_main:], ((0, 0), (0, GATE_RANK_PAD - GATE_RANK)))
        w_gk = jnp.pad(gla_w_gk[l], ((0, GATE_RANK_PAD - GATE_RANK), (0, 0)))
        u, v, q, k, vb, sg, gk = _in_proj(
            x2, norm_mix_pre[l][None], w_in[l], w_lr, sgu_ln_g[l][None], sgu_ln_b[l][None],
            w_gk, gla_b_gk[l][None])
        bias_full = jnp.repeat(jnp.transpose(sgu_b[l]), A_HEAD_DIM, axis=1)
        x2 = _mixer(x2, u, v, q, k, vb, sg, gk, sgu_w_s[l], bias_full, gla_norm_g[l][None],
                    w_out[l], norm_mix_post[l][None], bsz)
        x2 = _conv_ffn(x2, norm_ffn_pre[l][None], w_up[l].astype(BF16), conv_w[l],
                       conv_b[l][None], w_down[l], norm_ffn_post[l][None], bsz)
    return x2.reshape(bsz, s, d)
```

```python
import functools
import math

import jax
import jax.numpy as jnp
from jax import lax
from jax.experimental import pallas as pl
from jax.experimental.pallas import tpu as pltpu

F32 = jnp.float32
BF16 = jnp.bfloat16

EPS = 1e-6
A_HEADS = 4
A_HEAD_DIM = 128
A_WIDTH = A_HEADS * A_HEAD_DIM
SGU_CHUNK = 128
B_HEADS = 4
B_HEAD_V = 128
B_HEAD_K = 64
B_WIDTH_V = B_HEADS * B_HEAD_V
B_WIDTH_K = B_HEADS * B_HEAD_K
GATE_RANK = 16
GATE_TAU = 16.0
CONV_K = 3

LANES = 128
SUBLANES = 8
GATE_RANK_PAD = LANES
GLA_BLOCK = 256
FF_BLOCK = 256
DOWN_GROUP = 4
GLA_MIDPOINT_MAX_DECAY = 60.0
VMEM_LIMIT = 56 * 1024 * 1024

IN_TILE = 1024
MIX_TILE = 1024
FFN_TILE = 512


def _rms_norm(x, g):
    ms = jnp.mean(x * x, axis=-1, keepdims=True)
    return x * lax.rsqrt(ms + EPS) * g


def _gelu_exact(x):
    return 0.5 * x * (1.0 + lax.erf(x * (1.0 / math.sqrt(2.0))))


def _gelu_tanh(x):
    c = math.sqrt(2.0 / math.pi)
    return 0.5 * x * (1.0 + jnp.tanh(c * (x + 0.044715 * (x * x * x))))


def _log_sigmoid(x):
    return jnp.minimum(x, 0.0) - jnp.log1p(jnp.exp(-jnp.abs(x)))


def _dot(a, b):
    return jnp.dot(a, b, preferred_element_type=F32)


def _dot_nt(a, b):
    return lax.dot_general(a, b, (((1,), (1,)), ((), ())), preferred_element_type=F32)


def _dot_tn(a, b):
    return lax.dot_general(a, b, (((0,), (0,)), ((), ())), preferred_element_type=F32)


def _in_proj_kernel(x_ref, g_ref, wt_ref, lng_ref, lnb_ref, wgkt_ref, bgk_ref,
                    u_ref, v_ref, q_ref, k_ref, vb_ref, sg_ref, gk_ref, wg_ref, wbf_ref):
    n_main = wbf_ref.shape[0]
    @pl.when(pl.program_id(0) == 0)
    def _():
        tail = wt_ref[wt_ref.shape[0] - GATE_RANK_PAD:, :]
        wg_ref[...] = jnp.dot(wgkt_ref[...], tail, preferred_element_type=F32,
                              precision=lax.Precision.HIGHEST).astype(BF16)
        wbf_ref[...] = wt_ref[:n_main, :].astype(BF16)

    h = _rms_norm(x_ref[...], g_ref[...]).astype(BF16)
    off_u = 0
    off_v = off_u + A_WIDTH
    off_q = off_v + A_WIDTH
    off_k = off_q + B_WIDTH_K
    off_vb = off_k + B_WIDTH_K
    off_g = off_vb + B_WIDTH_V

    def proj(off, width):
        return _dot_nt(h, wbf_ref[off:off + width, :])

    u_ref[...] = _gelu_exact(proj(off_u, A_WIDTH)).astype(BF16)

    v = _gelu_exact(proj(off_v, A_WIDTH))
    mu = jnp.mean(v, axis=-1, keepdims=True)
    vc = v - mu
    var = jnp.mean(vc * vc, axis=-1, keepdims=True)
    v_ref[...] = (vc * lax.rsqrt(var + EPS) * lng_ref[...] + lnb_ref[...]).astype(BF16)

    g_out = proj(off_g, B_WIDTH_V)
    sg_ref[...] = (g_out * jax.nn.sigmoid(g_out)).astype(BF16)

    pre = _dot_nt(h, wg_ref[...]) + bgk_ref[...]
    gk_ref[...] = _log_sigmoid(pre) * (1.0 / GATE_TAU)

    vb_ref[...] = proj(off_vb, B_WIDTH_V).astype(BF16)
    q_ref[...] = proj(off_q, B_WIDTH_K).astype(BF16)
    k_ref[...] = proj(off_k, B_WIDTH_K).astype(BF16)


def _in_proj(x2, g, w_in_t, ln_g, ln_b, w_gk_t, b_gk):
    t, d = x2.shape
    tm = IN_TILE
    n_in = w_in_t.shape[0]
    n_main = n_in - GATE_RANK
    row = lambda i: (i, 0)
    const = lambda i: (0, 0)
    single = pl.Buffered(1)
    return pl.pallas_call(
        _in_proj_kernel,
        grid=(t // tm,),
        in_specs=[
            pl.BlockSpec((tm, d), row),
            pl.BlockSpec((1, d), const),
            pl.BlockSpec((n_in, d), const, pipeline_mode=single),
            pl.BlockSpec((1, A_WIDTH), const),
            pl.BlockSpec((1, A_WIDTH), const),
            pl.BlockSpec((B_WIDTH_K, GATE_RANK_PAD), const),
            pl.BlockSpec((1, B_WIDTH_K), const),
        ],
        out_specs=[
            pl.BlockSpec((tm, A_WIDTH), row),
            pl.BlockSpec((tm, A_WIDTH), row),
            pl.BlockSpec((tm, B_WIDTH_K), row),
            pl.BlockSpec((tm, B_WIDTH_K), row),
            pl.BlockSpec((tm, B_WIDTH_V), row),
            pl.BlockSpec((tm, B_WIDTH_V), row),
            pl.BlockSpec((tm, B_WIDTH_K), row),
        ],
        out_shape=[
            jax.ShapeDtypeStruct((t, A_WIDTH), BF16),
            jax.ShapeDtypeStruct((t, A_WIDTH), BF16),
            jax.ShapeDtypeStruct((t, B_WIDTH_K), BF16),
            jax.ShapeDtypeStruct((t, B_WIDTH_K), BF16),
            jax.ShapeDtypeStruct((t, B_WIDTH_V), BF16),
            jax.ShapeDtypeStruct((t, B_WIDTH_V), BF16),
            jax.ShapeDtypeStruct((t, B_WIDTH_K), F32),
        ],
        scratch_shapes=[pltpu.VMEM((B_WIDTH_K, d), BF16), pltpu.VMEM((n_main, d), BF16)],
        compiler_params=pltpu.CompilerParams(
            dimension_semantics=("arbitrary",), vmem_limit_bytes=VMEM_LIMIT),
        name="in_proj",
    )(x2, g, w_in_t, ln_g, ln_b, w_gk_t, b_gk)


def _gla_block(rows, states, safe, q_ref, k_ref, vb_ref, sg_ref, gk_ref, gng_ref, mix_ref):
    cb = GLA_BLOCK
    ri = lax.broadcasted_iota(jnp.int32, (cb, cb), 0)
    ci = lax.broadcasted_iota(jnp.int32, (cb, cb), 1)
    tri_incl = ri >= ci
    lane = lax.broadcasted_iota(jnp.int32, (1, LANES), 1)
    scale = B_HEAD_K ** -0.5

    def block_sum(mask, g_hi, g_lo):
        m = jnp.where(mask, 1.0, 0.0).astype(BF16)
        return _dot(m, g_hi) + _dot(m, g_lo)

    g = gk_ref[rows, :]
    g_hi = g.astype(BF16)
    g_lo = (g - g_hi.astype(F32)).astype(BF16)
    b = block_sum(tri_incl, g_hi, g_lo)
    b_last = b[cb - 1:cb, :]
    q = q_ref[rows, :].astype(F32) * scale
    k = k_ref[rows, :].astype(F32)
    s_decay = jnp.exp(b_last)

    def head_lanes(h):
        tile = slice((h // 2) * LANES, (h // 2 + 1) * LANES)
        in_head = (lane >= (h % 2) * B_HEAD_K) & (lane < (h % 2 + 1) * B_HEAD_K)
        return tile, in_head

    if not safe:
        mid = cb // 2 - 1
        b_mid = b[mid:mid + 1, :]
        qt = q * jnp.exp(b - b_mid)
        kt = k * jnp.exp(b_mid - b)
        q_in = (qt * jnp.exp(b_mid)).astype(BF16)
        k_dec = kt * jnp.exp(b_last - b_mid)
        qt = qt.astype(BF16)
        atts = []
        for h in range(B_HEADS):
            tile, in_head = head_lanes(h)
            k_m = jnp.where(in_head, kt[:, tile], 0.0).astype(BF16)
            atts.append(jnp.where(tri_incl, _dot_nt(qt[:, tile], k_m), 0.0))
    else:
        q_in = (q * jnp.exp(b)).astype(BF16)
        k_dec = k * jnp.exp(b_last - b)
        atts = []
        for h in range(B_HEADS):
            tile, in_head = head_lanes(h)
            k_m = jnp.where(in_head, k[:, tile], 0.0).astype(BF16)
            atts.append(jnp.where(ri == ci, _dot_nt(q[:, tile].astype(BF16), k_m), 0.0))
        s = cb // 2
        while s >= 1:
            same = (ri // s) == (ci // s)
            c_s = block_sum(same & tri_incl, g_hi, g_lo)
            t_s = block_sum(same, g_hi, g_lo)
            q_s = (q * jnp.exp(c_s)).astype(BF16)
            k_s = k * jnp.exp(t_s - c_s)
            pair = (((ri // s) % 2) == 1) & ((ci // s) == (ri // s) - 1)
            for h in range(B_HEADS):
                tile, in_head = head_lanes(h)
                k_m = jnp.where(in_head, k_s[:, tile], 0.0).astype(BF16)
                atts[h] = atts[h] + jnp.where(pair, _dot_nt(q_s[:, tile], k_m), 0.0)
            s //= 2

    new_states = []
    for h in range(B_HEADS):
        tile, in_head = head_lanes(h)
        kd_m = jnp.where(in_head, k_dec[:, tile], 0.0).astype(BF16)
        vcols = slice(h * B_HEAD_V, (h + 1) * B_HEAD_V)
        v_h = vb_ref[rows, vcols]
        o = _dot(atts[h].astype(BF16), v_h) + _dot_nt(q_in[:, tile], states[h].astype(BF16))
        new_states.append(states[h] * s_decay[:, tile] + _dot_tn(v_h, kd_m))
        o = _rms_norm(o, gng_ref[...]) * sg_ref[rows, vcols].astype(F32)
        mix_ref[rows, A_WIDTH + h * B_HEAD_V:A_WIDTH + (h + 1) * B_HEAD_V] = o.astype(BF16)
    return new_states


def _mixer_kernel(x_ref, u_ref, v_ref, q_ref, k_ref, vb_ref, sg_ref, gk_ref,
                  ws_ref, bias_ref, gng_ref, wout_ref, npost_ref,
                  o_ref, state_ref, mix_ref, wout_bf_ref):
    tm = x_ref.shape[0]
    cb = GLA_BLOCK
    n_blk = tm // cb

    @pl.when((pl.program_id(0) == 0) & (pl.program_id(1) == 0))
    def _():
        wout_bf_ref[...] = wout_ref[...].astype(BF16)

    @pl.when(pl.program_id(1) == 0)
    def _():
        state_ref[...] = jnp.zeros_like(state_ref)

    r = lax.broadcasted_iota(jnp.int32, (SGU_CHUNK, SGU_CHUNK), 0)
    c = lax.broadcasted_iota(jnp.int32, (SGU_CHUNK, SGU_CHUNK), 1)
    n_ch = tm // SGU_CHUNK
    for h in range(A_HEADS):
        cols = slice(h * A_HEAD_DIM, (h + 1) * A_HEAD_DIM)
        w_m = jnp.where(r >= c, ws_ref[h], 0.0).astype(BF16)
        v_all = jnp.concatenate(
            [v_ref[ch * SGU_CHUNK:(ch + 1) * SGU_CHUNK, cols] for ch in range(n_ch)], axis=1)
        z_all = _dot(w_m, v_all)
        for ch in range(n_ch):
            rows = slice(ch * SGU_CHUNK, (ch + 1) * SGU_CHUNK)
            z = z_all[:, ch * A_HEAD_DIM:(ch + 1) * A_HEAD_DIM] + bias_ref[:, cols]
            mix_ref[rows, cols] = (u_ref[rows, cols].astype(F32) * z).astype(BF16)

    worst = jnp.zeros((1, B_WIDTH_K), F32)
    for blk in range(n_blk):
        worst = jnp.maximum(worst, -jnp.sum(gk_ref[blk * cb:(blk + 1) * cb, :], axis=0, keepdims=True))
    bounded = jnp.max(worst) < GLA_MIDPOINT_MAX_DECAY

    gla_refs = (q_ref, k_ref, vb_ref, sg_ref, gk_ref, gng_ref, mix_ref)

    def merge():
        mix = _dot(mix_ref[...], wout_bf_ref[...])
        o_ref[...] = x_ref[...] + _rms_norm(mix, npost_ref[...])

    def fast():
        states = [state_ref[h] for h in range(B_HEADS)]
        for blk in range(n_blk):
            states = _gla_block(slice(blk * cb, (blk + 1) * cb), states, False, *gla_refs)
        for h in range(B_HEADS):
            state_ref[h] = states[h]
        merge()

    def slow():
        def body(blk, states):
            rows = pl.ds(pl.multiple_of(blk * cb, cb), cb)
            return tuple(_gla_block(rows, list(states), True, *gla_refs))
        states = lax.fori_loop(0, n_blk, body, tuple(state_ref[h] for h in range(B_HEADS)))
        for h in range(B_HEADS):
            state_ref[h] = states[h]
        merge()

    lax.cond(bounded, fast, slow)


def _mixer(x2, u, v, q, k, vb, sg, gk, w_s, bias_full, gn_g, w_out, n_post, batch):
    t, d = x2.shape
    tm = MIX_TILE
    per_b = t // batch // tm
    row = lambda b, i: (b * per_b + i, 0)
    const2 = lambda b, i: (0, 0)
    const3 = lambda b, i: (0, 0, 0)
    single = pl.Buffered(1)
    return pl.pallas_call(
        _mixer_kernel,
        grid=(batch, per_b),
        in_specs=[
            pl.BlockSpec((tm, d), row),
            pl.BlockSpec((tm, A_WIDTH), row),
            pl.BlockSpec((tm, A_WIDTH), row),
            pl.BlockSpec((tm, B_WIDTH_K), row),
            pl.BlockSpec((tm, B_WIDTH_K), row),
            pl.BlockSpec((tm, B_WIDTH_V), row),
            pl.BlockSpec((tm, B_WIDTH_V), row),
            pl.BlockSpec((tm, B_WIDTH_K), row),
            pl.BlockSpec((A_HEADS, SGU_CHUNK, SGU_CHUNK), const3),
            pl.BlockSpec((SGU_CHUNK, A_WIDTH), const2),
            pl.BlockSpec((1, B_HEAD_V), const2),
            pl.BlockSpec((d, d), const2, pipeline_mode=single),
            pl.BlockSpec((1, d), const2),
        ],
        out_specs=pl.BlockSpec((tm, d), row),
        out_shape=jax.ShapeDtypeStruct((t, d), F32),
        scratch_shapes=[
            pltpu.VMEM((B_HEADS, B_HEAD_V, LANES), F32),
            pltpu.VMEM((tm, d), BF16),
            pltpu.VMEM((d, d), BF16),
        ],
        compiler_params=pltpu.CompilerParams(
            dimension_semantics=("arbitrary", "arbitrary"), vmem_limit_bytes=VMEM_LIMIT),
        name="mixer",
    )(x2, u, v, q, k, vb, sg, gk, w_s, bias_full, gn_g, w_out, n_post)


def _perm_base(r, n_groups):
    per = n_groups // SUBLANES
    return (r % per) * SUBLANES * SUBLANES + r // per


def _conv_ffn_kernel(x_ref, gpre_ref, wup_ref, cw_ref, cb_ref, wdn_ref, gpost_ref,
                     o_ref, carry_ref, hperm_ref, rperm_ref, y_ref, wdn_bf_ref):
    tm, d = x_ref.shape
    d_ff = wdn_ref.shape[0]
    ng = tm // SUBLANES
    nslab = d // LANES
    nblk = d_ff // FF_BLOCK

    @pl.when((pl.program_id(0) == 0) & (pl.program_id(1) == 0))
    def _():
        wdn_bf_ref[...] = wdn_ref[...].astype(BF16)

    @pl.when(pl.program_id(1) == 0)
    def _():
        carry_ref[...] = jnp.zeros_like(carry_ref)

    h32 = _rms_norm(x_ref[...], gpre_ref[...])
    for r in range(ng):
        base = _perm_base(r, ng)
        for sl in range(nslab):
            hperm_ref[sl, pl.ds(base, SUBLANES, stride=SUBLANES), :] = (
                h32[r * SUBLANES:(r + 1) * SUBLANES, sl * LANES:(sl + 1) * LANES])
    h = jnp.concatenate([hperm_ref[sl] for sl in range(nslab)], axis=-1).astype(BF16)

    sub = lax.broadcasted_iota(jnp.int32, (SUBLANES, FF_BLOCK), 0)

    def up(j):
        ca = slice(j * FF_BLOCK, (j + 1) * FF_BLOCK)
        cg = slice(d_ff + j * FF_BLOCK, d_ff + (j + 1) * FF_BLOCK)
        y_ref[j % 2, 0] = _dot(h, wup_ref[:, ca])
        y_ref[j % 2, 1] = _dot(h, wup_ref[:, cg])

    def conv(j, part):
        cols = slice(part * d_ff + j * FF_BLOCK, part * d_ff + (j + 1) * FF_BLOCK)
        yr = y_ref.at[j % 2, part]
        y = yr[...]
        prev = carry_ref[:, cols]
        last2 = yr[tm - 2 * SUBLANES:tm, :]
        carry_ref[0:1, cols] = last2[SUBLANES - 1:SUBLANES]
        carry_ref[1:2, cols] = last2[2 * SUBLANES - 1:2 * SUBLANES]
        f2 = jnp.where(sub == 0, prev[0:1], pltpu.roll(last2[:SUBLANES], 1, 0))
        f1 = jnp.where(sub == 0, prev[1:2], pltpu.roll(last2[SUBLANES:], 1, 0))
        y1 = jnp.concatenate([f1, y[:tm - SUBLANES]], axis=0)
        y2 = jnp.concatenate([f2, f1, y[:tm - 2 * SUBLANES]], axis=0)
        w = cw_ref[:, cols]
        return cb_ref[:, cols] + y2 * w[0:1] + y1 * w[1:2] + y * w[2:3]

    acc = jnp.zeros((tm, d), F32)
    up(0)
    acts = []
    for j in range(nblk):
        if j + 1 < nblk:
            up(j + 1)
        acts.append((_gelu_tanh(conv(j, 0)) * conv(j, 1)).astype(BF16))
        if len(acts) == DOWN_GROUP or j == nblk - 1:
            lo = (j + 1 - len(acts)) * FF_BLOCK
            act = acts[0] if len(acts) == 1 else jnp.concatenate(acts, axis=-1)
            acc = acc + _dot(act, wdn_bf_ref[lo:(j + 1) * FF_BLOCK, :])
            acts = []

    res = _rms_norm(acc, gpost_ref[...])
    for sl in range(nslab):
        rperm_ref[sl] = res[:, sl * LANES:(sl + 1) * LANES]
    for r in range(ng):
        base = _perm_base(r, ng)
        for sl in range(nslab):
            rows = slice(r * SUBLANES, (r + 1) * SUBLANES)
            cols = slice(sl * LANES, (sl + 1) * LANES)
            o_ref[rows, cols] = (x_ref[rows, cols]
                                 + rperm_ref[sl, pl.ds(base, SUBLANES, stride=SUBLANES), :])


def _conv_ffn(x2, g_pre, w_up, conv_w, conv_b, w_down, g_post, batch):
    t, d = x2.shape
    tm = FFN_TILE
    per_b = t // batch // tm
    d_ff = w_down.shape[0]
    row = lambda b, i: (b * per_b + i, 0)
    const = lambda b, i: (0, 0)
    single = pl.Buffered(1)
    return pl.pallas_call(
        _conv_ffn_kernel,
        grid=(batch, per_b),
        in_specs=[
            pl.BlockSpec((tm, d), row),
            pl.BlockSpec((1, d), const),
            pl.BlockSpec((d, 2 * d_ff), const, pipeline_mode=single),
            pl.BlockSpec((CONV_K, 2 * d_ff), const),
            pl.BlockSpec((1, 2 * d_ff), const),
            pl.BlockSpec((d_ff, d), const, pipeline_mode=single),
            pl.BlockSpec((1, d), const),
        ],
        out_specs=pl.BlockSpec((tm, d), row),
        out_shape=jax.ShapeDtypeStruct((t, d), F32),
        scratch_shapes=[
            pltpu.VMEM((CONV_K - 1, 2 * d_ff), F32),
            pltpu.VMEM((d // LANES, tm, LANES), F32),
            pltpu.VMEM((d // LANES, tm, LANES), F32),
            pltpu.VMEM((2, 2, tm, FF_BLOCK), F32),
            pltpu.VMEM((d_ff, d), BF16),
        ],
        compiler_params=pltpu.CompilerParams(
            dimension_semantics=("arbitrary", "arbitrary"), vmem_limit_bytes=VMEM_LIMIT),
        name="conv_ffn",
    )(x2, g_pre, w_up, conv_w, conv_b, w_down, g_post)


def kernel(x, norm_mix_pre, w_in, sgu_ln_g, sgu_ln_b, sgu_w_s, sgu_b, gla_w_gk, gla_b_gk,
           gla_norm_g, w_out, norm_mix_post, norm_ffn_pre, w_up, conv_w, conv_b, w_down,
           norm_ffn_post):
    bsz, s, d = x.shape
    depth = w_in.shape[0]
    x2 = x.reshape(bsz * s, d)
    for l in range(depth):
        w_gk_t = jnp.pad(jnp.transpose(gla_w_gk[l]), ((0, 0), (GATE_RANK_PAD - GATE_RANK, 0)))
        u, v, q, k, vb, sg, gk = _in_proj(
            x2, norm_mix_pre[l][None], jnp.transpose(w_in[l]), sgu_ln_g[l][None], sgu_ln_b[l][None],
            w_gk_t, gla_b_gk[l][None])
        bias_full = jnp.repeat(jnp.transpose(sgu_b[l]), A_HEAD_DIM, axis=1)
        x2 = _mixer(x2, u, v, q, k, vb, sg, gk, sgu_w_s[l], bias_full, gla_norm_g[l][None],
                    w_out[l], norm_mix_post[l][None], bsz)
        x2 = _conv_ffn(x2, norm_ffn_pre[l][None], w_up[l].astype(BF16), conv_w[l],
                       conv_b[l][None], w_down[l], norm_ffn_post[l][None], bsz)
    return x2.reshape(bsz, s, d)
```
